```python
import jax, jax.numpy as jnp
from jax import lax
import numpy as np

D_MODEL = 1024
BATCH = 32
SEQ = 2048
DEPTH = 1

GRID_W = 64
HEAD_DIM = 128
LA_HEADS = 8
LA_DIM = LA_HEADS * HEAD_DIM
CONV_K = 5
CHUNK = 64
ATT_HEADS = 8
KV_HEADS = 2
GQA_GROUP = ATT_HEADS // KV_HEADS
ATT_Q_DIM = ATT_HEADS * HEAD_DIM
ATT_KV_DIM = KV_HEADS * HEAD_DIM
Q_BLOCK = 128
ROPE_THETA = 10000.0
MEM_TOKENS = 256
MEM_HEADS = 4
MEM_HEAD_DIM = D_MODEL // MEM_HEADS
D_FF = -(-(8 * D_MODEL) // (3 * 256)) * 256
EPS = 1e-6
IN_SPLITS = (3 * LA_DIM, LA_DIM, 2 * LA_HEADS, 2 * LA_HEADS,
             ATT_Q_DIM, ATT_KV_DIM, ATT_KV_DIM, 2 * D_MODEL)
IN_DIM = 3 * LA_DIM + LA_DIM + 4 * LA_HEADS + ATT_Q_DIM + 2 * ATT_KV_DIM + 2 * D_MODEL

kernel_name = "hybrid_gdn_axial_gqa_memory_encoder"


def _rms_norm(x, w):
    xf = x.astype(jnp.float32)
    y = xf * lax.rsqrt(jnp.mean(xf * xf, axis=-1, keepdims=True) + EPS)
    return (y * w.astype(jnp.float32)).astype(x.dtype)


def _l2_norm(x):
    xf = x.astype(jnp.float32)
    return xf * lax.rsqrt(jnp.sum(xf * xf, axis=-1, keepdims=True) + EPS)


def _split_cols(a, sizes):
    out, start = [], 0
    for s in sizes:
        out.append(a[..., start:start + s])
        start += s
    return out


def _centred_depthwise_conv(x, w):
    pad = (CONV_K - 1) // 2
    return lax.conv_general_dilated(
        x, w[:, None, :].astype(x.dtype), window_strides=(1,), padding=[(pad, pad)],
        dimension_numbers=("NWC", "WIO", "NWC"), feature_group_count=x.shape[-1])


def _gated_delta_chunked(q, k, v, g, beta):
    B, H, T, D = q.shape
    N = T // CHUNK
    q = (q * D ** -0.5).reshape(B, H, N, CHUNK, D)
    k = k.reshape(B, H, N, CHUNK, D)
    v = v.reshape(B, H, N, CHUNK, D)
    g = jnp.cumsum(g.reshape(B, H, N, CHUNK), axis=-1)
    beta = beta.reshape(B, H, N, CHUNK)
    k_beta = k * beta[..., None]
    v_beta = v * beta[..., None]
    tri = jnp.tril(jnp.ones((CHUNK, CHUNK), dtype=bool))
    strict = jnp.tril(jnp.ones((CHUNK, CHUNK), dtype=bool), -1)
    diff = g[..., :, None] - g[..., None, :]
    decay = jnp.where(tri, jnp.exp(jnp.where(tri, diff, 0.0)), 0.0)
    L = jnp.where(strict, jnp.einsum("bhncd,bhnsd->bhncs", k_beta, k) * decay, 0.0)
    eye = jnp.eye(CHUNK, dtype=q.dtype)
    t_inv = lax.linalg.triangular_solve(eye + L, jnp.broadcast_to(eye, L.shape),
                                        left_side=True, lower=True, unit_diagonal=True)
    u = jnp.einsum("bhncs,bhnsd->bhncd", t_inv, v_beta)
    w = jnp.einsum("bhncs,bhnsd->bhncd", t_inv, k_beta * jnp.exp(g)[..., None])
    qk = jnp.where(tri, jnp.einsum("bhncd,bhnsd->bhncs", q, k) * decay, 0.0)
    g_last = g[..., -1]
    q_g = q * jnp.exp(g)[..., None]
    k_d = k * jnp.exp(g_last[..., None] - g)[..., None]
    d_last = jnp.exp(g_last)

    def step(S, xs):
        qg_c, kd_c, u_c, w_c, qk_c, dl_c = xs
        v_new = u_c - jnp.einsum("bhcd,bhde->bhce", w_c, S)
        o = jnp.einsum("bhcd,bhde->bhce", qg_c, S) + jnp.einsum("bhcs,bhse->bhce", qk_c, v_new)
        S = S * dl_c[..., None, None] + jnp.einsum("bhcd,bhce->bhde", kd_c, v_new)
        return S, o

    xs = tuple(jnp.moveaxis(a, 2, 0) for a in (q_g, k_d, u, w, qk, d_last))
    S0 = jnp.zeros((B, H, D, D), q.dtype)
    _, o = lax.scan(step, S0, xs)
    return jnp.moveaxis(o, 0, 2).reshape(B, H, T, D)


def _mixer_a(qkv, z, a_raw, b_raw, conv_w, a_log, dt_bias, norm_w, w_out_a):
    B, T, _ = qkv.shape
    f32 = jnp.float32
    qkv = jax.nn.silu(_centred_depthwise_conv(qkv, conv_w))
    q, k, v = jnp.split(qkv, 3, axis=-1)
    heads = lambda a: a.reshape(B, T, LA_HEADS, HEAD_DIM).transpose(0, 2, 1, 3)
    q = _l2_norm(heads(q))
    k = _l2_norm(heads(k))
    v = heads(v).astype(f32)
    dirs = lambda a: a.astype(f32).reshape(B, T, 2, LA_HEADS).transpose(2, 0, 3, 1)
    g = -jnp.exp(a_log.astype(f32))[:, None, :, None] * jax.nn.softplus(
        dirs(a_raw) + dt_bias.astype(f32)[:, None, :, None])
    beta = jax.nn.sigmoid(dirs(b_raw))
    o_fwd = _gated_delta_chunked(q, k, v, g[0], beta[0])
    flip = lambda a: jnp.flip(a, axis=2)
    o_bwd = flip(_gated_delta_chunked(flip(q), flip(k), flip(v),
                                      jnp.flip(g[1], axis=-1), jnp.flip(beta[1], axis=-1)))
    o = (o_fwd + o_bwd).transpose(0, 2, 1, 3)
    o = _rms_norm(o, norm_w) * jax.nn.silu(z.reshape(B, T, LA_HEADS, HEAD_DIM).astype(f32))
    return o.reshape(B, T, LA_DIM).astype(qkv.dtype) @ w_out_a


def _axial_rope_tables(T, dtype):
    rows = T // GRID_W
    row = jnp.repeat(jnp.arange(rows, dtype=jnp.float32), GRID_W)
    col = jnp.tile(jnp.arange(GRID_W, dtype=jnp.float32), rows)
    half = HEAD_DIM // 2
    inv_freq = ROPE_THETA ** (-jnp.arange(0, half, 2, dtype=jnp.float32) / half)
    ang_r = (row[:, None] * inv_freq)[:, None, :]
    ang_c = (col[:, None] * inv_freq)[:, None, :]
    return (jnp.cos(ang_r).astype(dtype), jnp.sin(ang_r).astype(dtype),
            jnp.cos(ang_c).astype(dtype), jnp.sin(ang_c).astype(dtype))


def _rotate(x, cos, sin):
    x1, x2 = jnp.split(x, 2, axis=-1)
    return jnp.concatenate([x1 * cos - x2 * sin, x2 * cos + x1 * sin], axis=-1)


def _axial_rope(x, cos_r, sin_r, cos_c, sin_c):
    x_row, x_col = jnp.split(x, 2, axis=-1)
    return jnp.concatenate([_rotate(x_row, cos_r, sin_r), _rotate(x_col, cos_c, sin_c)], axis=-1)


def _block_attention(q, k, v):
    B, T = q.shape[:2]
    nb = T // Q_BLOCK
    qb = q.reshape(B, nb, Q_BLOCK, KV_HEADS, GQA_GROUP, HEAD_DIM).transpose(1, 0, 2, 3, 4, 5)
    scale = HEAD_DIM ** -0.5

    def one_block(q_blk):
        s = jnp.einsum("bqkgd,btkd->bkgqt", q_blk, k).astype(jnp.float32) * scale
        p = jax.nn.softmax(s, axis=-1).astype(v.dtype)
        return jnp.einsum("bkgqt,btkd->bqkgd", p, v)

    o = lax.map(one_block, qb)
    return o.transpose(1, 0, 2, 3, 4, 5).reshape(B, T, ATT_Q_DIM)


def _mixer_b(q, k, v, q_norm_w, k_norm_w, w_out_b):
    B, T, _ = q.shape
    q = _rms_norm(q.reshape(B, T, ATT_HEADS, HEAD_DIM), q_norm_w)
    k = _rms_norm(k.reshape(B, T, KV_HEADS, HEAD_DIM), k_norm_w)
    v = v.reshape(B, T, KV_HEADS, HEAD_DIM)
    tables = _axial_rope_tables(T, q.dtype)
    q = _axial_rope(q, *tables).reshape(B, T, KV_HEADS, GQA_GROUP, HEAD_DIM)
    k = _axial_rope(k, *tables)
    return _block_attention(q, k, v) @ w_out_b


def _memory_attention(h, mem_n, w_mq, w_mkv, w_mo):
    B, T, _ = h.shape
    M = mem_n.shape[1]
    q = (h @ w_mq).reshape(B, T, MEM_HEADS, MEM_HEAD_DIM)
    k, v = jnp.split(mem_n @ w_mkv, 2, axis=-1)
    k = k.reshape(B, M, MEM_HEADS, MEM_HEAD_DIM)
    v = v.reshape(B, M, MEM_HEADS, MEM_HEAD_DIM)
    s = jnp.einsum("bthd,bmhd->bhtm", q, k).astype(jnp.float32) * MEM_HEAD_DIM ** -0.5
    p = jax.nn.softmax(s, axis=-1).astype(v.dtype)
    o = jnp.einsum("bhtm,bmhd->bthd", p, v).reshape(B, T, D_MODEL)
    return o @ w_mo


def _swiglu(h, w_in, w_out):
    gate, up = jnp.split(h @ w_in, 2, axis=-1)
    return (jax.nn.silu(gate) * up) @ w_out


def setup_inputs(seed: int = 0) -> dict:
    key = jax.random.key(seed)
    ks = iter(jax.random.split(key, 40))
    f32 = jnp.float32

    def dense(fan_in, fan_out):
        return jax.random.normal(next(ks), (DEPTH, fan_in, fan_out), f32) * fan_in ** -0.5

    def gain(n):
        return 1.0 + 0.02 * jax.random.normal(next(ks), (DEPTH, n), f32)

    x = jax.random.normal(next(ks), (BATCH, SEQ, D_MODEL), f32)
    mem = jax.random.normal(next(ks), (BATCH, MEM_TOKENS, D_MODEL), f32)
    mix_pre_norm = gain(D_MODEL)
    w_in = dense(D_MODEL, IN_DIM)
    conv_w = jax.random.normal(next(ks), (DEPTH, CONV_K, 3 * LA_DIM), f32) * CONV_K ** -0.5
    la_a_log = jnp.log(jax.random.uniform(next(ks), (DEPTH, 2, LA_HEADS), f32, 1.0, 16.0))
    dt = jnp.exp(jax.random.uniform(next(ks), (DEPTH, 2, LA_HEADS), f32,
                                    float(np.log(1e-3)), float(np.log(1e-1))))
    la_dt_bias = dt + jnp.log(-jnp.expm1(-dt))
    la_norm_w = gain(HEAD_DIM)
    w_out_a = dense(LA_DIM, D_MODEL)
    q_norm_w = gain(HEAD_DIM)
    k_norm_w = gain(HEAD_DIM)
    w_out_b = dense(ATT_Q_DIM, D_MODEL)
    b_gate = 0.01 * jax.random.normal(next(ks), (DEPTH, 2 * D_MODEL), f32)
    w_out = dense(D_MODEL, D_MODEL)
    mix_post_norm = gain(D_MODEL)
    mem_pre_norm = gain(D_MODEL)
    mem_kv_norm = gain(D_MODEL)
    w_mq = dense(D_MODEL, D_MODEL)
    w_mkv = dense(D_MODEL, 2 * D_MODEL)
    w_mo = dense(D_MODEL, D_MODEL)
    mem_post_norm = gain(D_MODEL)
    ffn_pre_norm = gain(D_MODEL)
    w_ffn_in = dense(D_MODEL, 2 * D_FF)
    w_ffn_out = dense(D_FF, D_MODEL)
    ffn_post_norm = gain(D_MODEL)
    return {"x": x, "mem": mem,
            "mix_pre_norm": mix_pre_norm, "w_in": w_in, "conv_w": conv_w,
            "la_a_log": la_a_log, "la_dt_bias": la_dt_bias, "la_norm_w": la_norm_w,
            "w_out_a": w_out_a, "q_norm_w": q_norm_w, "k_norm_w": k_norm_w,
            "w_out_b": w_out_b, "b_gate": b_gate, "w_out": w_out, "mix_post_norm": mix_post_norm,
            "mem_pre_norm": mem_pre_norm, "mem_kv_norm": mem_kv_norm, "w_mq": w_mq,
            "w_mkv": w_mkv, "w_mo": w_mo, "mem_post_norm": mem_post_norm,
            "ffn_pre_norm": ffn_pre_norm, "w_ffn_in": w_ffn_in, "w_ffn_out": w_ffn_out,
            "ffn_post_norm": ffn_post_norm}


def reference(x, mem, mix_pre_norm, w_in, conv_w, la_a_log, la_dt_bias, la_norm_w, w_out_a,
              q_norm_w, k_norm_w, w_out_b, b_gate, w_out, mix_post_norm,
              mem_pre_norm, mem_kv_norm, w_mq, w_mkv, w_mo, mem_post_norm,
              ffn_pre_norm, w_ffn_in, w_ffn_out, ffn_post_norm):
    for l in range(DEPTH):
        h = _rms_norm(x, mix_pre_norm[l])
        proj = h @ w_in[l]
        qkv_a, z_a, a_raw, b_raw, q_b, k_b, v_b, gates = _split_cols(proj, IN_SPLITS)
        y_a = _mixer_a(qkv_a, z_a, a_raw, b_raw, conv_w[l], la_a_log[l], la_dt_bias[l],
                       la_norm_w[l], w_out_a[l])
        y_b = _mixer_b(q_b, k_b, v_b, q_norm_w[l], k_norm_w[l], w_out_b[l])
        g_a, g_b = jnp.split(jax.nn.sigmoid(gates + b_gate[l]), 2, axis=-1)
        mixed = (g_a * y_a + g_b * y_b) @ w_out[l]
        x = x + _rms_norm(mixed, mix_post_norm[l])
        h = _rms_norm(x, mem_pre_norm[l])
        mem_n = _rms_norm(mem, mem_kv_norm[l])
        x = x + _rms_norm(_memory_attention(h, mem_n, w_mq[l], w_mkv[l], w_mo[l]), mem_post_norm[l])
        h = _rms_norm(x, ffn_pre_norm[l])
        x = x + _rms_norm(_swiglu(h, w_ffn_in[l], w_ffn_out[l]), ffn_post_norm[l])
    return x
```

```python
import functools

import jax
import jax.numpy as jnp
from jax import lax
from jax.experimental import pallas as pl
from jax.experimental.pallas import tpu as pltpu

F32 = jnp.float32
BF16 = jnp.bfloat16

D_MODEL = 1024
HEAD_DIM = 128
LA_HEADS = 8
LA_DIM = LA_HEADS * HEAD_DIM
CONV_K = 5
CHUNK = 64
ATT_HEADS = 8
KV_HEADS = 2
GQA_GROUP = ATT_HEADS // KV_HEADS
ATT_Q_DIM = ATT_HEADS * HEAD_DIM
ATT_KV_DIM = KV_HEADS * HEAD_DIM
GRID_W = 64
ROPE_THETA = 10000.0
MEM_HEADS = 4
MEM_HEAD_DIM = D_MODEL // MEM_HEADS
EPS = 1e-6

LANES = 128
VMEM_LIMIT = 56 * 1024 * 1024

C_QKV = 0
C_Z = C_QKV + 3 * LA_DIM
C_QB = C_Z + LA_DIM
C_KB = C_QB + ATT_Q_DIM
C_VB = C_KB + ATT_KV_DIM
C_GATE = C_VB + ATT_KV_DIM
C_AB = C_GATE + 2 * D_MODEL
N_PROJ = C_AB + LANES


def _mm(a, b):
    return jnp.dot(a.astype(BF16), b.astype(BF16), preferred_element_type=F32)


def _mm_nt(a, b):
    return lax.dot_general(a.astype(BF16), b.astype(BF16), (((1,), (1,)), ((), ())),
                           preferred_element_type=F32)


def _mm_tn(a, b):
    return lax.dot_general(a.astype(BF16), b.astype(BF16), (((0,), (0,)), ((), ())),
                           preferred_element_type=F32)


def _rms(x, w):
    return x * lax.rsqrt(jnp.mean(x * x, axis=-1, keepdims=True) + EPS) * w


def _sigmoid(x):
    return 1.0 / (1.0 + jnp.exp(-x))


def _silu(x):
    return x * _sigmoid(x)


def _resident(shape):
    nd = len(shape)
    return pl.BlockSpec(shape, lambda *_: (0,) * nd, pipeline_mode=pl.Buffered(1))


def _params(n_grid):
    return pltpu.CompilerParams(dimension_semantics=("arbitrary",) * n_grid,
                                vmem_limit_bytes=VMEM_LIMIT)


def _proj_kernel(x_ref, nw_ref, w_ref, bg_ref, qn_ref, kn_ref, cos_ref, sin_ref, alog_ref, dtb_ref,
                 qkv_ref, z_ref, qb_ref, kb_ref, vb_ref, gate_ref, gt_ref):
    tm = x_ref.shape[0]
    h = _rms(x_ref[...], nw_ref[...]).astype(BF16)

    def mm(c0, n):
        return jnp.dot(h, w_ref[:, c0:c0 + n], preferred_element_type=F32)

    step = 512
    for c in range(0, 3 * LA_DIM, step):
        qkv_ref[:, c:c + step] = mm(C_QKV + c, step)
    for c in range(0, LA_DIM, step):
        z_ref[:, c:c + step] = mm(C_Z + c, step)
    for c in range(0, 2 * D_MODEL, step):
        gate_ref[:, c:c + step] = _sigmoid(mm(C_GATE + c, step) + bg_ref[:, c:c + step])
    vb_ref[...] = mm(C_VB, ATT_KV_DIM).astype(BF16)

    cos = cos_ref[...]
    sin = sin_ref[...]
    lane = lax.broadcasted_iota(jnp.int32, (tm, LANES), 1)
    first = (lane % 64) < 32

    def norm_rope(y, nw, scale):
        y = _rms(y, nw)
        partner = jnp.where(first, pltpu.roll(y, 96, axis=1), pltpu.roll(y, 32, axis=1))
        return ((y * cos + partner * sin) * scale).astype(BF16)

    qn = qn_ref[...]
    for c in range(0, ATT_Q_DIM, step):
        r = mm(C_QB + c, step)
        for j in range(0, step, HEAD_DIM):
            qb_ref[:, c + j:c + j + HEAD_DIM] = norm_rope(r[:, j:j + HEAD_DIM], qn, HEAD_DIM ** -0.5)
    kn = kn_ref[...]
    r = mm(C_KB, ATT_KV_DIM)
    for j in range(0, ATT_KV_DIM, HEAD_DIM):
        kb_ref[:, j:j + HEAD_DIM] = norm_rope(r[:, j:j + HEAD_DIM], kn, 1.0)

    r = mm(C_AB, LANES)
    xs = r + dtb_ref[...]
    softplus = jnp.maximum(xs, 0.0) + jnp.log1p(jnp.exp(-jnp.abs(xs)))
    g = -jnp.exp(alog_ref[...]) * softplus
    pos = lax.broadcasted_iota(jnp.int32, (tm, LANES), 0) % CHUNK
    pre = g
    suf = g
    s = 1
    while s < CHUNK:
        pre = pre + jnp.where(pos >= s, pltpu.roll(pre, s, axis=0), 0.0)
        suf = suf + jnp.where(pos < CHUNK - s, pltpu.roll(suf, tm - s, axis=0), 0.0)
        s *= 2
    gt_ref[...] = jnp.where(lane < LA_HEADS, pre, jnp.where(lane < 2 * LA_HEADS, suf, _sigmoid(r)))


def _proj(x2, nw, w, bg, qn, kn, cos_t, sin_t, alog, dtb, T, tm):
    M = x2.shape[0]
    tpb = T // tm
    row = lambda n: pl.BlockSpec((tm, n), lambda i: (i, 0))
    tab = pl.BlockSpec((tm, LANES), lambda i: (i % tpb, 0))
    outs = [(3 * LA_DIM, F32), (LA_DIM, F32), (ATT_Q_DIM, BF16), (ATT_KV_DIM, BF16),
            (ATT_KV_DIM, BF16), (2 * D_MODEL, F32), (LANES, F32)]
    return pl.pallas_call(
        _proj_kernel,
        grid=(M // tm,),
        in_specs=[row(D_MODEL), _resident((1, D_MODEL)), _resident((D_MODEL, N_PROJ)),
                  _resident((1, 2 * D_MODEL)), _resident((1, HEAD_DIM)), _resident((1, HEAD_DIM)),
                  tab, tab, _resident((1, LANES)), _resident((1, LANES))],
        out_specs=[row(n) for n, _ in outs],
        out_shape=[jax.ShapeDtypeStruct((M, n), dt) for n, dt in outs],
        compiler_params=_params(1),
        name="proj",
    )(x2, nw, w, bg, qn, kn, cos_t, sin_t, alog, dtb)


def _conv_silu(x_ref, w_ref):
    x = x_ref[...]
    w = w_ref[...]
    T = x.shape[0]
    row = lax.broadcasted_iota(jnp.int32, x.shape, 0)
    pad = (CONV_K - 1) // 2
    acc = x * w[pad:pad + 1, :]
    for j in range(CONV_K):
        s = j - pad
        if s == 0:
            continue
        shifted = pltpu.roll(x, (-s) % T, axis=0)
        valid = (row + s >= 0) & (row + s < T)
        acc = acc + jnp.where(valid, shifted, 0.0) * w[j:j + 1, :]
    return _silu(acc)


def _l2n(x):
    return x * lax.rsqrt(jnp.sum(x * x, axis=-1, keepdims=True) + EPS)


def _delta_chunk(q, k, v, gtc, gr, S, lane_g, lane_b, lower):
    C = CHUNK
    lane = lax.broadcasted_iota(jnp.int32, (C, LANES), 1)
    gc = jnp.sum(jnp.where(lane == lane_g, gtc, 0.0), axis=1, keepdims=True)
    beta = jnp.sum(jnp.where(lane == lane_b, gtc, 0.0), axis=1, keepdims=True)
    ii = lax.broadcasted_iota(jnp.int32, (C, C), 0)
    jj = lax.broadcasted_iota(jnp.int32, (C, C), 1)
    incl = (ii >= jj) if lower else (ii <= jj)
    strict = (ii > jj) if lower else (ii < jj)
    decay = jnp.where(incl, jnp.exp(jnp.where(incl, gc - gr, 0.0)), 0.0)
    kb = k * beta
    aq = _mm_nt(jnp.concatenate([kb, q], axis=0), k)
    neg_l = jnp.where(strict, -(aq[:C] * decay), 0.0)
    qk = aq[C:] * decay
    eye = jnp.where(ii == jj, 1.0, 0.0)
    p = _mm(neg_l, neg_l)
    a = eye + neg_l
    n_terms = 2
    while n_terms < C // 2:
        r = _mm(p, jnp.concatenate([p, a], axis=1))
        p = r[:, :C]
        a = a + r[:, C:]
        n_terms *= 2
    eg = jnp.exp(gc)
    y = _mm(a, jnp.concatenate([v * beta, kb * eg], axis=1))
    y = y + _mm(p, y)
    u = y[:, :HEAD_DIM]
    w = y[:, HEAD_DIM:]
    ws = _mm(jnp.concatenate([w, q * eg], axis=0), S)
    v_new = u - ws[:C]
    o = ws[C:] + _mm(qk, v_new)
    gl = gr[:, C - 1:C] if lower else gr[:, 0:1]
    kd = k * jnp.exp(gl - gc)
    S = S * jnp.exp(gl) + _mm_tn(kd, v_new)
    return o, S


def _gdn_kernel(q_ref, k_ref, v_ref, wq_ref, wk_ref, wv_ref, z_ref, gt_ref, gtr_ref, nw_ref,
                o_ref, qs_ref, ks_ref, vs_ref, of_ref, ob_ref):
    T = q_ref.shape[0]
    N = T // CHUNK
    hd = pl.program_id(1)
    qs_ref[...] = _l2n(_conv_silu(q_ref, wq_ref)) * (HEAD_DIM ** -0.5)
    ks_ref[...] = _l2n(_conv_silu(k_ref, wk_ref))
    vs_ref[...] = _conv_silu(v_ref, wv_ref)

    def body(n, carry):
        s_f, s_b = carry
        cf = pl.multiple_of(n * CHUNK, CHUNK)
        nb = N - 1 - n
        cb = pl.multiple_of(nb * CHUNK, CHUNK)
        o_f, s_f = _delta_chunk(qs_ref[pl.ds(cf, CHUNK), :], ks_ref[pl.ds(cf, CHUNK), :],
                                vs_ref[pl.ds(cf, CHUNK), :], gt_ref[pl.ds(cf, CHUNK), :],
                                gtr_ref[n, pl.ds(hd, 1), :], s_f, hd, 2 * LA_HEADS + hd, True)
        o_b, s_b = _delta_chunk(qs_ref[pl.ds(cb, CHUNK), :], ks_ref[pl.ds(cb, CHUNK), :],
                                vs_ref[pl.ds(cb, CHUNK), :], gt_ref[pl.ds(cb, CHUNK), :],
                                gtr_ref[nb, pl.ds(LA_HEADS + hd, 1), :], s_b,
                                LA_HEADS + hd, 3 * LA_HEADS + hd, False)
        of_ref[pl.ds(cf, CHUNK), :] = o_f
        ob_ref[pl.ds(cb, CHUNK), :] = o_b
        return s_f, s_b

    zero = jnp.zeros((HEAD_DIM, HEAD_DIM), F32)
    lax.fori_loop(0, N, body, (zero, zero))
    o = _rms(of_ref[...] + ob_ref[...], nw_ref[...]) * _silu(z_ref[...])
    o_ref[...] = o.astype(BF16)


def _gdn(qkv, conv_w, z, gt, gtr, nw):
    B, T, _ = qkv.shape
    N = T // CHUNK
    seq = lambda off: pl.BlockSpec((None, T, HEAD_DIM), lambda b, h: (b, 0, off + h))
    cw = lambda off: pl.BlockSpec((CONV_K, HEAD_DIM), lambda b, h: (0, off + h))
    return pl.pallas_call(
        _gdn_kernel,
        grid=(B, LA_HEADS),
        in_specs=[seq(0), seq(LA_HEADS), seq(2 * LA_HEADS),
                  cw(0), cw(LA_HEADS), cw(2 * LA_HEADS),
                  seq(0),
                  pl.BlockSpec((None, T, LANES), lambda b, h: (b, 0, 0)),
                  pl.BlockSpec((None, N, 4 * LA_HEADS, CHUNK), lambda b, h: (b, 0, 0, 0)),
                  pl.BlockSpec((1, HEAD_DIM), lambda b, h: (0, 0))],
        out_specs=seq(0),
        out_shape=jax.ShapeDtypeStruct((B, T, LA_DIM), BF16),
        scratch_shapes=[pltpu.VMEM((T, HEAD_DIM), F32)] * 5,
        compiler_params=_params(2),
        name="gdn",
    )(qkv, qkv, qkv, conv_w, conv_w, conv_w, z, gt, gtr, nw)


def _attn_kernel(q_ref, k_ref, v_ref, o_ref):
    k = k_ref[...]
    v = v_ref[...]
    for g in range(GQA_GROUP):
        sl = slice(g * HEAD_DIM, (g + 1) * HEAD_DIM)
        s = _mm_nt(q_ref[:, sl], k)
        p = jnp.exp(s - jnp.max(s, axis=-1, keepdims=True))
        l = jnp.sum(p, axis=-1, keepdims=True)
        o_ref[:, sl] = (_mm(p, v) / l).astype(BF16)


def _attn(qb, kb, vb, tq):
    B, T, _ = qb.shape
    gw = GQA_GROUP * HEAD_DIM
    return pl.pallas_call(
        _attn_kernel,
        grid=(B, KV_HEADS, T // tq),
        in_specs=[pl.BlockSpec((None, tq, gw), lambda b, kv, i: (b, i, kv)),
                  pl.BlockSpec((None, T, HEAD_DIM), lambda b, kv, i: (b, 0, kv)),
                  pl.BlockSpec((None, T, HEAD_DIM), lambda b, kv, i: (b, 0, kv))],
        out_specs=pl.BlockSpec((None, tq, gw), lambda b, kv, i: (b, i, kv)),
        out_shape=jax.ShapeDtypeStruct((B, T, ATT_Q_DIM), BF16),
        compiler_params=_params(3),
        name="attn",
    )(qb, kb, vb)


def _merge_kernel(x_ref, oa_ref, ob_ref, gate_ref, wa_ref, wb_ref, wo_ref, nw_ref, o_ref):
    ya = jnp.dot(oa_ref[...], wa_ref[...], preferred_element_type=F32)
    yb = jnp.dot(ob_ref[...], wb_ref[...], preferred_element_type=F32)
    mix = gate_ref[:, :D_MODEL] * ya + gate_ref[:, D_MODEL:] * yb
    mixed = jnp.dot(mix.astype(BF16), wo_ref[...], preferred_element_type=F32)
    o_ref[...] = x_ref[...] + _rms(mixed, nw_ref[...])


def _merge(x2, oa, ob, gate, wa, wb, wo, nw, tm):
    M = x2.shape[0]
    row = lambda n: pl.BlockSpec((tm, n), lambda i: (i, 0))
    sq = _resident((D_MODEL, D_MODEL))
    return pl.pallas_call(
        _merge_kernel,
        grid=(M // tm,),
        in_specs=[row(D_MODEL), row(D_MODEL), row(D_MODEL), row(2 * D_MODEL), sq, sq, sq,
                  _resident((1, D_MODEL))],
        out_specs=row(D_MODEL),
        out_shape=jax.ShapeDtypeStruct((M, D_MODEL), F32),
        compiler_params=_params(1),
        name="merge",
    )(x2, oa, ob, gate, wa, wb, wo, nw)


def _memkv_kernel(mem_ref, nw_ref, w_ref, o_ref):
    o_ref[...] = jnp.dot(_rms(mem_ref[...], nw_ref[...]).astype(BF16), w_ref[...],
                         preferred_element_type=F32).astype(BF16)


def _memkv(mem, nw, w):
    B, Mt, _ = mem.shape
    return pl.pallas_call(
        _memkv_kernel,
        grid=(B,),
        in_specs=[pl.BlockSpec((None, Mt, D_MODEL), lambda b: (b, 0, 0)),
                  _resident((1, D_MODEL)), _resident((D_MODEL, 2 * D_MODEL))],
        out_specs=pl.BlockSpec((None, Mt, 2 * D_MODEL), lambda b: (b, 0, 0)),
        out_shape=jax.ShapeDtypeStruct((B, Mt, 2 * D_MODEL), BF16),
        compiler_params=_params(1),
        name="memkv",
    )(mem, nw, w)


def _memattn_kernel(x_ref, kv_ref, prew_ref, wq_ref, wo_ref, postw_ref, o_ref, att_ref):
    x = x_ref[...]
    h = _rms(x, prew_ref[...]).astype(BF16)
    q = (jnp.dot(h, wq_ref[...], preferred_element_type=F32) * MEM_HEAD_DIM ** -0.5).astype(BF16)
    for hd in range(MEM_HEADS):
        sl = slice(hd * MEM_HEAD_DIM, (hd + 1) * MEM_HEAD_DIM)
        s = _mm_nt(q[:, sl], kv_ref[:, sl])
        p = jnp.exp(s - jnp.max(s, axis=-1, keepdims=True))
        l = jnp.sum(p, axis=-1, keepdims=True)
        v = kv_ref[:, D_MODEL + hd * MEM_HEAD_DIM:D_MODEL + (hd + 1) * MEM_HEAD_DIM]
        att_ref[:, sl] = (_mm(p, v) / l).astype(BF16)
    y = jnp.dot(att_ref[...], wo_ref[...], preferred_element_type=F32)
    o_ref[...] = x + _rms(y, postw_ref[...])


def _memattn(x2, kv, prew, wq, wo, postw, T, tm):
    M = x2.shape[0]
    Mt = kv.shape[1]
    tpb = T // tm
    row = pl.BlockSpec((tm, D_MODEL), lambda i: (i, 0))
    sq = _resident((D_MODEL, D_MODEL))
    vec = _resident((1, D_MODEL))
    return pl.pallas_call(
        _memattn_kernel,
        grid=(M // tm,),
        in_specs=[row, pl.BlockSpec((None, Mt, 2 * D_MODEL), lambda i: (i // tpb, 0, 0)),
                  vec, sq, sq, vec],
        out_specs=row,
        out_shape=jax.ShapeDtypeStruct((M, D_MODEL), F32),
        scratch_shapes=[pltpu.VMEM((tm, D_MODEL), BF16)],
        compiler_params=_params(1),
        name="memattn",
    )(x2, kv, prew, wq, wo, postw)


def _ffn_kernel(x_ref, prew_ref, wg_ref, wu_ref, wd_ref, postw_ref, o_ref, *, n_split):
    x = x_ref[...]
    h = _rms(x, prew_ref[...]).astype(BF16)
    d_ff = wg_ref.shape[1]
    cs = d_ff // n_split
    y = None
    for c in range(0, d_ff, cs):
        gate = jnp.dot(h, wg_ref[:, c:c + cs], preferred_element_type=F32)
        up = jnp.dot(h, wu_ref[:, c:c + cs], preferred_element_type=F32)
        part = jnp.dot((_silu(gate) * up).astype(BF16), wd_ref[c:c + cs, :],
                       preferred_element_type=F32)
        y = part if y is None else y + part
    o_ref[...] = x + _rms(y, postw_ref[...])


def _ffn(x2, prew, wg, wu, wd, postw, tm):
    M = x2.shape[0]
    d_ff = wg.shape[1]
    row = pl.BlockSpec((tm, D_MODEL), lambda i: (i, 0))
    vec = _resident((1, D_MODEL))
    return pl.pallas_call(
        functools.partial(_ffn_kernel, n_split=2),
        grid=(M // tm,),
        in_specs=[row, vec, _resident((D_MODEL, d_ff)), _resident((D_MODEL, d_ff)),
                  _resident((d_ff, D_MODEL)), vec],
        out_specs=row,
        out_shape=jax.ShapeDtypeStruct((M, D_MODEL), F32),
        compiler_params=_params(1),
        name="ffn",
    )(x2, prew, wg, wu, wd, postw)


def _rope_tables(T):
    rows = T // GRID_W
    row = jnp.repeat(jnp.arange(rows, dtype=F32), GRID_W)
    col = jnp.tile(jnp.arange(GRID_W, dtype=F32), rows)
    half = HEAD_DIM // 2
    inv_freq = ROPE_THETA ** (-jnp.arange(0, half, 2, dtype=F32) / half)
    ang_r = row[:, None] * inv_freq
    ang_c = col[:, None] * inv_freq
    cos_t = jnp.concatenate([jnp.cos(ang_r), jnp.cos(ang_r), jnp.cos(ang_c), jnp.cos(ang_c)], axis=-1)
    sin_t = jnp.concatenate([-jnp.sin(ang_r), jnp.sin(ang_r), -jnp.sin(ang_c), jnp.sin(ang_c)], axis=-1)
    return cos_t, sin_t


def _layer(x, mem, mix_pre_norm, w_in, conv_w, la_a_log, la_dt_bias, la_norm_w, w_out_a,
           q_norm_w, k_norm_w, w_out_b, b_gate, w_out, mix_post_norm,
           mem_pre_norm, mem_kv_norm, w_mq, w_mkv, w_mo, mem_post_norm,
           ffn_pre_norm, w_ffn_in, w_ffn_out, ffn_post_norm):
    B, T, D = x.shape
    assert D == D_MODEL and T % 256 == 0 and T % GRID_W == 0
    M = B * T
    N = T // CHUNK
    tm_proj = 256
    tm_tok = 512 if T % 512 == 0 else 256
    vec = lambda a: a.reshape(1, -1).astype(F32)

    a0 = 4 * LA_DIM
    b0 = a0 + 4 * LA_HEADS
    w_r = jnp.concatenate([w_in[:, :a0], w_in[:, b0:], w_in[:, a0:b0],
                           jnp.zeros((D, LANES - 4 * LA_HEADS), w_in.dtype)], axis=1).astype(BF16)
    pad16 = lambda a: jnp.concatenate([a.reshape(-1).astype(F32), jnp.zeros((LANES - 2 * LA_HEADS,), F32)]).reshape(1, LANES)
    cos_t, sin_t = _rope_tables(T)

    x2 = x.reshape(M, D)
    qkv, z, qb, kb, vb, gate, gt = _proj(x2, vec(mix_pre_norm), w_r, vec(b_gate), vec(q_norm_w),
                                         vec(k_norm_w), cos_t, sin_t, pad16(la_a_log),
                                         pad16(la_dt_bias), T, tm_proj)
    gt3 = gt.reshape(B, T, LANES)
    gtr = gt3[:, :, :4 * LA_HEADS].reshape(B, N, CHUNK, 4 * LA_HEADS).transpose(0, 1, 3, 2)
    oa = _gdn(qkv.reshape(B, T, 3 * LA_DIM), conv_w.astype(F32), z.reshape(B, T, LA_DIM), gt3, gtr,
              vec(la_norm_w))
    ob = _attn(qb.reshape(B, T, ATT_Q_DIM), kb.reshape(B, T, ATT_KV_DIM),
               vb.reshape(B, T, ATT_KV_DIM), 256)
    x2 = _merge(x2, oa.reshape(M, LA_DIM), ob.reshape(M, ATT_Q_DIM), gate, w_out_a.astype(BF16),
                w_out_b.astype(BF16), w_out.astype(BF16), vec(mix_post_norm), tm_tok)
    kv = _memkv(mem, vec(mem_kv_norm), w_mkv.astype(BF16))
    x2 = _memattn(x2, kv, vec(mem_pre_norm), w_mq.astype(BF16), w_mo.astype(BF16),
                  vec(mem_post_norm), T, tm_tok)
    d_ff = w_ffn_out.shape[0]
    x2 = _ffn(x2, vec(ffn_pre_norm), w_ffn_in[:, :d_ff].astype(BF16), w_ffn_in[:, d_ff:].astype(BF16),
              w_ffn_out.astype(BF16), vec(ffn_post_norm), tm_tok)
    return x2.reshape(B, T, D)


def kernel(x, mem, mix_pre_norm, w_in, conv_w, la_a_log, la_dt_bias, la_norm_w, w_out_a, q_norm_w, k_norm_w, w_out_b, b_gate, w_out, mix_post_norm, mem_pre_norm, mem_kv_norm, w_mq, w_mkv, w_mo, mem_post_norm, ffn_pre_norm, w_ffn_in, w_ffn_out, ffn_post_norm):
    depth = w_in.shape[0]
    for l in range(depth):
        x = _layer(x, mem, mix_pre_norm[l], w_in[l], conv_w[l], la_a_log[l], la_dt_bias[l],
                   la_norm_w[l], w_out_a[l], q_norm_w[l], k_norm_w[l], w_out_b[l], b_gate[l],
                   w_out[l], mix_post_norm[l], mem_pre_norm[l], mem_kv_norm[l], w_mq[l], w_mkv[l],
                   w_mo[l], mem_post_norm[l], ffn_pre_norm[l], w_ffn_in[l], w_ffn_out[l],
                   ffn_post_norm[l])
    return x
```

```python
import functools

import jax
import jax.numpy as jnp
from jax import lax
from jax.experimental import pallas as pl
from jax.experimental.pallas import tpu as pltpu

F32 = jnp.float32
BF16 = jnp.bfloat16

D_MODEL = 1024
HEAD_DIM = 128
LA_HEADS = 8
LA_DIM = LA_HEADS * HEAD_DIM
CONV_K = 5
CHUNK = 64
ATT_HEADS = 8
KV_HEADS = 2
GQA_GROUP = ATT_HEADS // KV_HEADS
ATT_Q_DIM = ATT_HEADS * HEAD_DIM
ATT_KV_DIM = KV_HEADS * HEAD_DIM
GRID_W = 64
ROPE_THETA = 10000.0
MEM_HEADS = 4
MEM_HEAD_DIM = D_MODEL // MEM_HEADS
EPS = 1e-6

LANES = 128
VMEM_LIMIT = 56 * 1024 * 1024

C_QKV = 0
C_Z = C_QKV + 3 * LA_DIM
C_QB = C_Z + LA_DIM
C_KB = C_QB + ATT_Q_DIM
C_VB = C_KB + ATT_KV_DIM
C_GATE = C_VB + ATT_KV_DIM
C_AB = C_GATE + 2 * D_MODEL
N_PROJ = C_AB + LANES


def _mm(a, b):
    return jnp.dot(a.astype(BF16), b.astype(BF16), preferred_element_type=F32)


def _mm_nt(a, b):
    return lax.dot_general(a.astype(BF16), b.astype(BF16), (((1,), (1,)), ((), ())),
                           preferred_element_type=F32)


def _mm_tn(a, b):
    return lax.dot_general(a.astype(BF16), b.astype(BF16), (((0,), (0,)), ((), ())),
                           preferred_element_type=F32)


def _rms(x, w):
    return x * lax.rsqrt(jnp.mean(x * x, axis=-1, keepdims=True) + EPS) * w


def _sigmoid(x):
    return 1.0 / (1.0 + jnp.exp(-x))


def _silu(x):
    return x * _sigmoid(x)


def _resident(shape):
    nd = len(shape)
    return pl.BlockSpec(shape, lambda *_: (0,) * nd, pipeline_mode=pl.Buffered(1))


def _params(n_grid):
    return pltpu.CompilerParams(dimension_semantics=("arbitrary",) * n_grid,
                                vmem_limit_bytes=VMEM_LIMIT)


def _proj_kernel(x_ref, nw_ref, w_ref, bg_ref, qn_ref, kn_ref, cos_ref, sin_ref, alog_ref, dtb_ref,
                 qkv_ref, z_ref, qb_ref, kb_ref, vb_ref, gate_ref, gt_ref):
    tm = x_ref.shape[0]
    h = _rms(x_ref[...], nw_ref[...]).astype(BF16)

    def mm(c0, n):
        return jnp.dot(h, w_ref[:, c0:c0 + n], preferred_element_type=F32)

    step = 512
    for c in range(0, 3 * LA_DIM, step):
        qkv_ref[:, c:c + step] = mm(C_QKV + c, step)
    for c in range(0, LA_DIM, step):
        z_ref[:, c:c + step] = mm(C_Z + c, step)
    for c in range(0, 2 * D_MODEL, step):
        gate_ref[:, c:c + step] = _sigmoid(mm(C_GATE + c, step) + bg_ref[:, c:c + step])
    vb_ref[...] = mm(C_VB, ATT_KV_DIM).astype(BF16)

    cos = cos_ref[...]
    sin = sin_ref[...]
    lane = lax.broadcasted_iota(jnp.int32, (tm, LANES), 1)
    first = (lane % 64) < 32

    def norm_rope(y, nw, scale):
        y = _rms(y, nw)
        partner = jnp.where(first, pltpu.roll(y, 96, axis=1), pltpu.roll(y, 32, axis=1))
        return ((y * cos + partner * sin) * scale).astype(BF16)

    qn = qn_ref[...]
    for c in range(0, ATT_Q_DIM, step):
        r = mm(C_QB + c, step)
        for j in range(0, step, HEAD_DIM):
            qb_ref[:, c + j:c + j + HEAD_DIM] = norm_rope(r[:, j:j + HEAD_DIM], qn, HEAD_DIM ** -0.5)
    kn = kn_ref[...]
    r = mm(C_KB, ATT_KV_DIM)
    for j in range(0, ATT_KV_DIM, HEAD_DIM):
        kb_ref[:, j:j + HEAD_DIM] = norm_rope(r[:, j:j + HEAD_DIM], kn, 1.0)

    r = mm(C_AB, LANES)
    xs = r + dtb_ref[...]
    softplus = jnp.maximum(xs, 0.0) + jnp.log1p(jnp.exp(-jnp.abs(xs)))
    g = -jnp.exp(alog_ref[...]) * softplus
    pos = lax.broadcasted_iota(jnp.int32, (tm, LANES), 0) % CHUNK
    pre = g
    suf = g
    s = 1
    while s < CHUNK:
        pre = pre + jnp.where(pos >= s, pltpu.roll(pre, s, axis=0), 0.0)
        suf = suf + jnp.where(pos < CHUNK - s, pltpu.roll(suf, tm - s, axis=0), 0.0)
        s *= 2
    gt_ref[...] = jnp.where(lane < LA_HEADS, pre, jnp.where(lane < 2 * LA_HEADS, suf, _sigmoid(r)))


def _proj(x2, nw, w, bg, qn, kn, cos_t, sin_t, alog, dtb, T, tm):
    M = x2.shape[0]
    tpb = T // tm
    row = lambda n: pl.BlockSpec((tm, n), lambda i: (i, 0))
    tab = pl.BlockSpec((tm, LANES), lambda i: (i % tpb, 0))
    outs = [(3 * LA_DIM, F32), (LA_DIM, F32), (ATT_Q_DIM, BF16), (ATT_KV_DIM, BF16),
            (ATT_KV_DIM, BF16), (2 * D_MODEL, F32), (LANES, F32)]
    return pl.pallas_call(
        _proj_kernel,
        grid=(M // tm,),
        in_specs=[row(D_MODEL), _resident((1, D_MODEL)), _resident((D_MODEL, N_PROJ)),
                  _resident((1, 2 * D_MODEL)), _resident((1, HEAD_DIM)), _resident((1, HEAD_DIM)),
                  tab, tab, _resident((1, LANES)), _resident((1, LANES))],
        out_specs=[row(n) for n, _ in outs],
        out_shape=[jax.ShapeDtypeStruct((M, n), dt) for n, dt in outs],
        compiler_params=_params(1),
        name="proj",
    )(x2, nw, w, bg, qn, kn, cos_t, sin_t, alog, dtb)


def _conv_silu_rows(x_ref, w_ref, sl, it, n_it, rows):
    T = x_ref.shape[0]
    halo = 8
    r0 = pl.multiple_of(it * rows, rows)
    w = w_ref[:, sl]
    x = x_ref[pl.ds(r0, rows), sl]
    before = x_ref[pl.ds(pl.multiple_of(jnp.maximum(r0 - halo, 0), halo), halo), sl]
    after = x_ref[pl.ds(pl.multiple_of(jnp.minimum(r0 + rows, T - halo), halo), halo), sl]
    before = jnp.where(it > 0, before, 0.0)
    after = jnp.where(it < n_it - 1, after, 0.0)
    xe = jnp.concatenate([before, x, after], axis=0)
    pad = (CONV_K - 1) // 2
    acc = x * w[pad:pad + 1, :]
    for j in range(CONV_K):
        s = j - pad
        if s == 0:
            continue
        shifted = pltpu.roll(xe, (-s) % (rows + 2 * halo), axis=0)[halo:halo + rows]
        acc = acc + shifted * w[j:j + 1, :]
    return _silu(acc)


def _l2n(x):
    return x * lax.rsqrt(jnp.sum(x * x, axis=-1, keepdims=True) + EPS)


def _chunk_local(probs):
    C = CHUNK
    n = len(probs)
    assert 2 * C == LANES
    lane = lax.broadcasted_iota(jnp.int32, (C, LANES), 1)
    ii = lax.broadcasted_iota(jnp.int32, (C, LANES), 0)
    jj = lane % C
    hi = lane >= C
    eye_hi = jnp.where((ii == jj) & hi, 1.0, 0.0)
    incl = [(ii >= jj) if p["lower"] else (ii <= jj) for p in probs]
    strict = [(ii > jj) if p["lower"] else (ii < jj) for p in probs]
    col = lambda p, l: jnp.sum(jnp.where(lane == l, p["gtc"], 0.0), axis=1, keepdims=True)
    gc = [col(p, p["lane_g"]) for p in probs]
    beta = [col(p, p["lane_b"]) for p in probs]
    decay = [jnp.where(incl[i], jnp.exp(jnp.where(incl[i], gc[i] - probs[i]["gr"], 0.0)), 0.0)
             for i in range(n)]
    kb = [probs[i]["k"] * beta[i] for i in range(n)]
    gram = {}
    for p in probs:
        if p["cid"] not in gram:
            kk = jnp.concatenate([p["k"], p["k"]], axis=0).astype(BF16)
            gram[p["cid"]] = lax.dot_general(
                jnp.concatenate([p["k"], p["q"]], axis=0).astype(BF16), kk,
                (((1,), (1,)), ((), ())), preferred_element_type=F32)
    aq = [gram[p["cid"]] for p in probs]
    neg_l = [jnp.where(strict[i], -(aq[i][:C] * beta[i] * decay[i]), 0.0) for i in range(n)]
    qk = [(aq[i][C:] * decay[i])[:, :C].astype(BF16) for i in range(n)]
    nb = [x.astype(BF16) for x in neg_l]
    sq = [jnp.dot(x[:, :C], x, preferred_element_type=F32) for x in nb]
    z = [jnp.where(hi, eye_hi + neg_l[i], sq[i]) for i in range(n)]
    n_terms = 2
    while n_terms < C // 2:
        zb = [x.astype(BF16) for x in z]
        r = [jnp.dot(x[:, :C], x, preferred_element_type=F32) for x in zb]
        z = [r[i] + jnp.where(hi, z[i], 0.0) for i in range(n)]
        n_terms *= 2
    zb = [x.astype(BF16) for x in z]
    pw = [x[:, :C] for x in zb]
    eg = [jnp.exp(x) for x in gc]
    rhs = [jnp.concatenate([probs[i]["v"] * beta[i], kb[i] * eg[i]], axis=1).astype(BF16)
           for i in range(n)]
    y = [jnp.dot(zb[i][:, C:], rhs[i], preferred_element_type=F32) for i in range(n)]
    y = [(y[i] + jnp.dot(pw[i], y[i].astype(BF16), preferred_element_type=F32)).astype(BF16)
         for i in range(n)]
    gl = [p["gr"][:, C - 1:C] if p["lower"] else p["gr"][:, 0:1] for p in probs]
    kd = [(probs[i]["k"] * jnp.exp(gl[i] - gc[i])).astype(BF16) for i in range(n)]
    r1 = [lax.dot_general(kd[i], y[i], (((0,), (0,)), ((), ())), preferred_element_type=F32)
          for i in range(n)]
    r2 = [jnp.dot(qk[i], y[i], preferred_element_type=F32) for i in range(n)]
    out = []
    for i in range(n):
        qp = probs[i]["q"] * eg[i] - r2[i][:, HEAD_DIM:]
        kq = jnp.concatenate([r1[i][:, HEAD_DIM:], qp], axis=0).astype(BF16)
        out.append((kq, r1[i][:, :HEAD_DIM], r2[i][:, :HEAD_DIM], jnp.exp(gl[i])))
    return out


GDN_HP = 2
GDN_CPI = 4
KQ_ROWS = HEAD_DIM + CHUNK


def _gdn_kernel(q_ref, k_ref, v_ref, wq_ref, wk_ref, wv_ref, z_ref, gt_ref, gtr_ref, nw_ref,
                o_ref, kq_ref, bm_ref, oc_ref, dc_ref):
    T = q_ref.shape[0]
    N = T // CHUNK
    n_it = N // GDN_CPI
    rows = GDN_CPI * CHUNK
    hd0 = pl.program_id(1) * GDN_HP

    def local_body(it, carry):
        probs, dest = [], []
        for hp in range(GDN_HP):
            hd = hd0 + hp
            sl = slice(hp * HEAD_DIM, (hp + 1) * HEAD_DIM)
            q_all = _l2n(_conv_silu_rows(q_ref, wq_ref, sl, it, n_it, rows)) * (HEAD_DIM ** -0.5)
            k_all = _l2n(_conv_silu_rows(k_ref, wk_ref, sl, it, n_it, rows))
            v_all = _conv_silu_rows(v_ref, wv_ref, sl, it, n_it, rows)
            for j in range(GDN_CPI):
                c = it * GDN_CPI + j
                tok = pl.multiple_of(c * CHUNK, CHUNK)
                q, k, v = (a[j * CHUNK:(j + 1) * CHUNK] for a in (q_all, k_all, v_all))
                gtc = gt_ref[pl.ds(tok, CHUNK), :]
                for d in range(2):
                    probs.append(dict(cid=(hp, j), q=q, k=k, v=v, gtc=gtc,
                                      gr=gtr_ref[c, pl.ds(d * LA_HEADS + hd, 1), :],
                                      lane_g=d * LA_HEADS + hd, lane_b=(2 + d) * LA_HEADS + hd,
                                      lower=(d == 0)))
                    dest.append(((hp * 2 + d) * N + c, (hp * 2 + d) * T + c * CHUNK))
        for (kq, bm, o0, dec), (slot, orow) in zip(_chunk_local(probs), dest):
            kq_ref[pl.ds(pl.multiple_of(slot * KQ_ROWS, CHUNK), KQ_ROWS), :] = kq
            bm_ref[pl.ds(pl.multiple_of(slot * HEAD_DIM, HEAD_DIM), HEAD_DIM), :] = bm
            oc_ref[pl.ds(pl.multiple_of(orow, CHUNK), CHUNK), :] = o0
            dc_ref[pl.ds(pl.multiple_of(slot * 8, 8), 8), :] = jnp.broadcast_to(dec, (8, LANES))
        return carry

    lax.fori_loop(0, n_it, local_body, 0)

    def state_body(n, states):
        slots, orows = [], []
        for ch in range(2 * GDN_HP):
            c = n if ch % 2 == 0 else N - 1 - n
            slots.append(ch * N + c)
            orows.append(pl.ds(pl.multiple_of(ch * T + c * CHUNK, CHUNK), CHUNK))
        r = [jnp.dot(kq_ref[pl.ds(pl.multiple_of(slots[ch] * KQ_ROWS, CHUNK), KQ_ROWS), :],
                     states[ch].astype(BF16), preferred_element_type=F32)
             for ch in range(2 * GDN_HP)]
        new = []
        for ch in range(2 * GDN_HP):
            dec = dc_ref[pl.ds(pl.multiple_of(slots[ch] * 8, 8), 8), :][0:1, :]
            bm = bm_ref[pl.ds(pl.multiple_of(slots[ch] * HEAD_DIM, HEAD_DIM), HEAD_DIM), :]
            new.append(states[ch] * dec + bm - r[ch][:HEAD_DIM])
            oc_ref[orows[ch], :] = oc_ref[orows[ch], :] + r[ch][HEAD_DIM:]
        return tuple(new)

    zero = jnp.zeros((HEAD_DIM, HEAD_DIM), F32)
    lax.fori_loop(0, N, state_body, (zero,) * (2 * GDN_HP))
    for hp in range(GDN_HP):
        sl = slice(hp * HEAD_DIM, (hp + 1) * HEAD_DIM)
        o = oc_ref[2 * hp * T:(2 * hp + 1) * T, :] + oc_ref[(2 * hp + 1) * T:(2 * hp + 2) * T, :]
        o_ref[:, sl] = (_rms(o, nw_ref[...]) * _silu(z_ref[:, sl])).astype(BF16)


def _gdn(qkv, conv_w, z, gt, gtr, nw):
    B, T, _ = qkv.shape
    N = T // CHUNK
    hw = GDN_HP * HEAD_DIM
    nblk = LA_HEADS // GDN_HP
    seq = lambda off: pl.BlockSpec((None, T, hw), lambda b, h: (b, 0, off + h))
    cw = lambda off: pl.BlockSpec((CONV_K, hw), lambda b, h: (0, off + h))
    chains = 2 * GDN_HP
    return pl.pallas_call(
        _gdn_kernel,
        grid=(B, nblk),
        in_specs=[seq(0), seq(nblk), seq(2 * nblk),
                  cw(0), cw(nblk), cw(2 * nblk),
                  seq(0),
                  pl.BlockSpec((None, T, LANES), lambda b, h: (b, 0, 0)),
                  pl.BlockSpec((None, N, 4 * LA_HEADS, LANES), lambda b, h: (b, 0, 0, 0)),
                  pl.BlockSpec((1, HEAD_DIM), lambda b, h: (0, 0))],
        out_specs=seq(0),
        out_shape=jax.ShapeDtypeStruct((B, T, LA_DIM), BF16),
        scratch_shapes=[
            pltpu.VMEM((chains * N * KQ_ROWS, HEAD_DIM), BF16),
            pltpu.VMEM((chains * N * HEAD_DIM, HEAD_DIM), F32),
            pltpu.VMEM((chains * T, HEAD_DIM), F32),
            pltpu.VMEM((chains * N * 8, LANES), F32)],
        compiler_params=_params(2),
        name="gdn",
    )(qkv, qkv, qkv, conv_w, conv_w, conv_w, z, gt, gtr, nw)


def _attn_kernel(q_ref, k_ref, v_ref, o_ref):
    k = k_ref[...]
    v = v_ref[...]
    for g in range(GQA_GROUP):
        sl = slice(g * HEAD_DIM, (g + 1) * HEAD_DIM)
        s = _mm_nt(q_ref[:, sl], k)
        p = jnp.exp(s - jnp.max(s, axis=-1, keepdims=True))
        l = jnp.sum(p, axis=-1, keepdims=True)
        o_ref[:, sl] = (_mm(p, v) / l).astype(BF16)


def _attn(qb, kb, vb, tq):
    B, T, _ = qb.shape
    gw = GQA_GROUP * HEAD_DIM
    return pl.pallas_call(
        _attn_kernel,
        grid=(B, KV_HEADS, T // tq),
        in_specs=[pl.BlockSpec((None, tq, gw), lambda b, kv, i: (b, i, kv)),
                  pl.BlockSpec((None, T, HEAD_DIM), lambda b, kv, i: (b, 0, kv)),
                  pl.BlockSpec((None, T, HEAD_DIM), lambda b, kv, i: (b, 0, kv))],
        out_specs=pl.BlockSpec((None, tq, gw), lambda b, kv, i: (b, i, kv)),
        out_shape=jax.ShapeDtypeStruct((B, T, ATT_Q_DIM), BF16),
        compiler_params=_params(3),
        name="attn",
    )(qb, kb, vb)


def _merge_kernel(x_ref, oa_ref, ob_ref, gate_ref, wa_ref, wb_ref, wo_ref, nw_ref, o_ref):
    ya = jnp.dot(oa_ref[...], wa_ref[...], preferred_element_type=F32)
    yb = jnp.dot(ob_ref[...], wb_ref[...], preferred_element_type=F32)
    mix = gate_ref[:, :D_MODEL] * ya + gate_ref[:, D_MODEL:] * yb
    mixed = jnp.dot(mix.astype(BF16), wo_ref[...], preferred_element_type=F32)
    o_ref[...] = x_ref[...] + _rms(mixed, nw_ref[...])


def _merge(x2, oa, ob, gate, wa, wb, wo, nw, tm):
    M = x2.shape[0]
    row = lambda n: pl.BlockSpec((tm, n), lambda i: (i, 0))
    sq = _resident((D_MODEL, D_MODEL))
    return pl.pallas_call(
        _merge_kernel,
        grid=(M // tm,),
        in_specs=[row(D_MODEL), row(D_MODEL), row(D_MODEL), row(2 * D_MODEL), sq, sq, sq,
                  _resident((1, D_MODEL))],
        out_specs=row(D_MODEL),
        out_shape=jax.ShapeDtypeStruct((M, D_MODEL), F32),
        compiler_params=_params(1),
        name="merge",
    )(x2, oa, ob, gate, wa, wb, wo, nw)


def _memkv_kernel(mem_ref, nw_ref, w_ref, o_ref):
    o_ref[...] = jnp.dot(_rms(mem_ref[...], nw_ref[...]).astype(BF16), w_ref[...],
                         preferred_element_type=F32).astype(BF16)


def _memkv(mem, nw, w):
    B, Mt, _ = mem.shape
    return pl.pallas_call(
        _memkv_kernel,
        grid=(B,),
        in_specs=[pl.BlockSpec((None, Mt, D_MODEL), lambda b: (b, 0, 0)),
                  _resident((1, D_MODEL)), _resident((D_MODEL, 2 * D_MODEL))],
        out_specs=pl.BlockSpec((None, Mt, 2 * D_MODEL), lambda b: (b, 0, 0)),
        out_shape=jax.ShapeDtypeStruct((B, Mt, 2 * D_MODEL), BF16),
        compiler_params=_params(1),
        name="memkv",
    )(mem, nw, w)


def _memattn_kernel(x_ref, kv_ref, prew_ref, wq_ref, wo_ref, postw_ref, o_ref, att_ref):
    x = x_ref[...]
    h = _rms(x, prew_ref[...]).astype(BF16)
    q = (jnp.dot(h, wq_ref[...], preferred_element_type=F32) * MEM_HEAD_DIM ** -0.5).astype(BF16)
    for hd in range(MEM_HEADS):
        sl = slice(hd * MEM_HEAD_DIM, (hd + 1) * MEM_HEAD_DIM)
        s = _mm_nt(q[:, sl], kv_ref[:, sl])
        p = jnp.exp(s - jnp.max(s, axis=-1, keepdims=True))
        l = jnp.sum(p, axis=-1, keepdims=True)
        v = kv_ref[:, D_MODEL + hd * MEM_HEAD_DIM:D_MODEL + (hd + 1) * MEM_HEAD_DIM]
        att_ref[:, sl] = (_mm(p, v) / l).astype(BF16)
    y = jnp.dot(att_ref[...], wo_ref[...], preferred_element_type=F32)
    o_ref[...] = x + _rms(y, postw_ref[...])


def _memattn(x2, kv, prew, wq, wo, postw, T, tm):
    M = x2.shape[0]
    Mt = kv.shape[1]
    tpb = T // tm
    row = pl.BlockSpec((tm, D_MODEL), lambda i: (i, 0))
    sq = _resident((D_MODEL, D_MODEL))
    vec = _resident((1, D_MODEL))
    return pl.pallas_call(
        _memattn_kernel,
        grid=(M // tm,),
        in_specs=[row, pl.BlockSpec((None, Mt, 2 * D_MODEL), lambda i: (i // tpb, 0, 0)),
                  vec, sq, sq, vec],
        out_specs=row,
        out_shape=jax.ShapeDtypeStruct((M, D_MODEL), F32),
        scratch_shapes=[pltpu.VMEM((tm, D_MODEL), BF16)],
        compiler_params=_params(1),
        name="memattn",
    )(x2, kv, prew, wq, wo, postw)


def _ffn_kernel(x_ref, prew_ref, wg_ref, wu_ref, wd_ref, postw_ref, o_ref, *, n_split):
    x = x_ref[...]
    h = _rms(x, prew_ref[...]).astype(BF16)
    d_ff = wg_ref.shape[1]
    cs = d_ff // n_split
    y = None
    for c in range(0, d_ff, cs):
        gate = jnp.dot(h, wg_ref[:, c:c + cs], preferred_element_type=F32)
        up = jnp.dot(h, wu_ref[:, c:c + cs], preferred_element_type=F32)
        part = jnp.dot((_silu(gate) * up).astype(BF16), wd_ref[c:c + cs, :],
                       preferred_element_type=F32)
        y = part if y is None else y + part
    o_ref[...] = x + _rms(y, postw_ref[...])


def _ffn(x2, prew, wg, wu, wd, postw, tm):
    M = x2.shape[0]
    d_ff = wg.shape[1]
    row = pl.BlockSpec((tm, D_MODEL), lambda i: (i, 0))
    vec = _resident((1, D_MODEL))
    return pl.pallas_call(
        functools.partial(_ffn_kernel, n_split=2),
        grid=(M // tm,),
        in_specs=[row, vec, _resident((D_MODEL, d_ff)), _resident((D_MODEL, d_ff)),
                  _resident((d_ff, D_MODEL)), vec],
        out_specs=row,
        out_shape=jax.ShapeDtypeStruct((M, D_MODEL), F32),
        compiler_params=_params(1),
        name="ffn",
    )(x2, prew, wg, wu, wd, postw)


def _rope_tables(T):
    rows = T // GRID_W
    row = jnp.repeat(jnp.arange(rows, dtype=F32), GRID_W)
    col = jnp.tile(jnp.arange(GRID_W, dtype=F32), rows)
    half = HEAD_DIM // 2
    inv_freq = ROPE_THETA ** (-jnp.arange(0, half, 2, dtype=F32) / half)
    ang_r = row[:, None] * inv_freq
    ang_c = col[:, None] * inv_freq
    cos_t = jnp.concatenate([jnp.cos(ang_r), jnp.cos(ang_r), jnp.cos(ang_c), jnp.cos(ang_c)], axis=-1)
    sin_t = jnp.concatenate([-jnp.sin(ang_r), jnp.sin(ang_r), -jnp.sin(ang_c), jnp.sin(ang_c)], axis=-1)
    return cos_t, sin_t


def _layer(x, mem, mix_pre_norm, w_in, conv_w, la_a_log, la_dt_bias, la_norm_w, w_out_a,
           q_norm_w, k_norm_w, w_out_b, b_gate, w_out, mix_post_norm,
           mem_pre_norm, mem_kv_norm, w_mq, w_mkv, w_mo, mem_post_norm,
           ffn_pre_norm, w_ffn_in, w_ffn_out, ffn_post_norm):
    B, T, D = x.shape
    assert D == D_MODEL and T % 256 == 0 and T % GRID_W == 0
    M = B * T
    N = T // CHUNK
    tm_proj = 256
    tm_tok = 512 if T % 512 == 0 else 256
    vec = lambda a: a.reshape(1, -1).astype(F32)

    a0 = 4 * LA_DIM
    b0 = a0 + 4 * LA_HEADS
    w_r = jnp.concatenate([w_in[:, :a0], w_in[:, b0:], w_in[:, a0:b0],
                           jnp.zeros((D, LANES - 4 * LA_HEADS), w_in.dtype)], axis=1).astype(BF16)
    pad16 = lambda a: jnp.concatenate([a.reshape(-1).astype(F32), jnp.zeros((LANES - 2 * LA_HEADS,), F32)]).reshape(1, LANES)
    cos_t, sin_t = _rope_tables(T)

    x2 = x.reshape(M, D)
    qkv, z, qb, kb, vb, gate, gt = _proj(x2, vec(mix_pre_norm), w_r, vec(b_gate), vec(q_norm_w),
                                         vec(k_norm_w), cos_t, sin_t, pad16(la_a_log),
                                         pad16(la_dt_bias), T, tm_proj)
    gt3 = gt.reshape(B, T, LANES)
    gtr = gt3[:, :, :4 * LA_HEADS].reshape(B, N, CHUNK, 4 * LA_HEADS).transpose(0, 1, 3, 2)
    gtr = jnp.concatenate([gtr, gtr], axis=-1)
    oa = _gdn(qkv.reshape(B, T, 3 * LA_DIM), conv_w.astype(F32), z.reshape(B, T, LA_DIM), gt3, gtr,
              vec(la_norm_w))
    ob = _attn(qb.reshape(B, T, ATT_Q_DIM), kb.reshape(B, T, ATT_KV_DIM),
               vb.reshape(B, T, ATT_KV_DIM), 256)
    x2 = _merge(x2, oa.reshape(M, LA_DIM), ob.reshape(M, ATT_Q_DIM), gate, w_out_a.astype(BF16),
                w_out_b.astype(BF16), w_out.astype(BF16), vec(mix_post_norm), tm_tok)
    kv = _memkv(mem, vec(mem_kv_norm), w_mkv.astype(BF16))
    x2 = _memattn(x2, kv, vec(mem_pre_norm), w_mq.astype(BF16), w_mo.astype(BF16),
                  vec(mem_post_norm), T, tm_tok)
    d_ff = w_ffn_out.shape[0]
    x2 = _ffn(x2, vec(ffn_pre_norm), w_ffn_in[:, :d_ff].astype(BF16), w_ffn_in[:, d_ff:].astype(BF16),
              w_ffn_out.astype(BF16), vec(ffn_post_norm), tm_tok)
    return x2.reshape(B, T, D)


def kernel(x, mem, mix_pre_norm, w_in, conv_w, la_a_log, la_dt_bias, la_norm_w, w_out_a, q_norm_w, k_norm_w, w_out_b, b_gate, w_out, mix_post_norm, mem_pre_norm, mem_kv_norm, w_mq, w_mkv, w_mo, mem_post_norm, ffn_pre_norm, w_ffn_in, w_ffn_out, ffn_post_norm):
    depth = w_in.shape[0]
    for l in range(depth):
        x = _layer(x, mem, mix_pre_norm[l], w_in[l], conv_w[l], la_a_log[l], la_dt_bias[l],
                   la_norm_w[l], w_out_a[l], q_norm_w[l], k_norm_w[l], w_out_b[l], b_gate[l],
                   w_out[l], mix_post_norm[l], mem_pre_norm[l], mem_kv_norm[l], w_mq[l], w_mkv[l],
                   w_mo[l], mem_post_norm[l], ffn_pre_norm[l], w_ffn_in[l], w_ffn_out[l],
                   ffn_post_norm[l])
    return x
```

```python
import functools

import jax
import jax.numpy as jnp
from jax import lax
from jax.experimental import pallas as pl
from jax.experimental.pallas import tpu as pltpu

F32 = jnp.float32
BF16 = jnp.bfloat16

D_MODEL = 1024
HEAD_DIM = 128
LA_HEADS = 8
LA_DIM = LA_HEADS * HEAD_DIM
CONV_K = 5
CHUNK = 64
ATT_HEADS = 8
KV_HEADS = 2
GQA_GROUP = ATT_HEADS // KV_HEADS
ATT_Q_DIM = ATT_HEADS * HEAD_DIM
ATT_KV_DIM = KV_HEADS * HEAD_DIM
GRID_W = 64
ROPE_THETA = 10000.0
MEM_HEADS = 4
MEM_HEAD_DIM = D_MODEL // MEM_HEADS
EPS = 1e-6

LANES = 128
VMEM_LIMIT = 56 * 1024 * 1024

C_QKV = 0
C_Z = C_QKV + 3 * LA_DIM
C_QB = C_Z + LA_DIM
C_KB = C_QB + ATT_Q_DIM
C_VB = C_KB + ATT_KV_DIM
C_GATE = C_VB + ATT_KV_DIM
C_AB = C_GATE + 2 * D_MODEL
N_PROJ = C_AB + LANES
PROJ_STEP = 256


def _mm(a, b):
    return jnp.dot(a.astype(BF16), b.astype(BF16), preferred_element_type=F32)


def _mm_nt(a, b):
    return lax.dot_general(a.astype(BF16), b.astype(BF16), (((1,), (1,)), ((), ())),
                           preferred_element_type=F32)


def _mm_tn(a, b):
    return lax.dot_general(a.astype(BF16), b.astype(BF16), (((0,), (0,)), ((), ())),
                           preferred_element_type=F32)


def _rms(x, w):
    return x * lax.rsqrt(jnp.mean(x * x, axis=-1, keepdims=True) + EPS) * w


def _sigmoid(x):
    return 1.0 / (1.0 + jnp.exp(-x))


def _silu(x):
    return x * _sigmoid(x)


def _resident(shape):
    nd = len(shape)
    return pl.BlockSpec(shape, lambda *_: (0,) * nd, pipeline_mode=pl.Buffered(1))


def _params(n_grid):
    return pltpu.CompilerParams(dimension_semantics=("arbitrary",) * n_grid,
                                vmem_limit_bytes=VMEM_LIMIT)


def _proj_kernel(x_ref, xp_ref, xn_ref, nw_ref, w_ref, cw_ref, bg_ref, qn_ref, kn_ref, cos_ref,
                 sin_ref, alog_ref, dtb_ref,
                 qkv_ref, z_ref, qb_ref, kb_ref, vb_ref, gate_ref, gt_ref, *, tiles_per_seq):
    tm = x_ref.shape[0]
    halo = xp_ref.shape[0]
    nw = nw_ref[...]
    h = _rms(x_ref[...], nw).astype(BF16)

    seq_tile = pl.program_id(0) % tiles_per_seq
    hp = jnp.where(seq_tile > 0, _rms(xp_ref[...], nw), 0.0).astype(BF16)
    hn = jnp.where(seq_tile < tiles_per_seq - 1, _rms(xn_ref[...], nw), 0.0).astype(BF16)
    he = jnp.concatenate([hp, h, hn], axis=0)
    pad = (CONV_K - 1) // 2
    step = PROJ_STEP
    cos = cos_ref[...]
    sin = sin_ref[...]
    lane = lax.broadcasted_iota(jnp.int32, (tm, LANES), 1)
    first = (lane % 64) < 32

    def norm_rope(y, nw, scale):
        y = _rms(y, nw)
        partner = jnp.where(first, pltpu.roll(y, 96, axis=1), pltpu.roll(y, 32, axis=1))
        return ((y * cos + partner * sin) * scale).astype(BF16)

    def qkv_epilogue(re, c):
        cw = cw_ref[:, c:c + step]
        acc = re[halo:halo + tm] * cw[pad:pad + 1, :]
        for j in range(CONV_K):
            s = j - pad
            if s != 0:
                shifted = pltpu.roll(re, (-s) % (tm + 2 * halo), axis=0)[halo:halo + tm]
                acc = acc + shifted * cw[j:j + 1, :]
        y = _silu(acc)
        for j in range(0, step, HEAD_DIM):
            col = c + j
            yh = y[:, j:j + HEAD_DIM]
            if col < 2 * LA_DIM:
                yh = yh * lax.rsqrt(jnp.sum(yh * yh, axis=-1, keepdims=True) + EPS)
            if col < LA_DIM:
                yh = yh * (HEAD_DIM ** -0.5)
            qkv_ref[:, col:col + HEAD_DIM] = yh.astype(BF16)

    def z_epilogue(r, c):
        z_ref[:, c:c + step] = r

    def gate_epilogue(r, c):
        gate_ref[:, c:c + step] = _sigmoid(r + bg_ref[:, c:c + step])

    def vb_epilogue(r, c):
        vb_ref[...] = r.astype(BF16)

    def qb_epilogue(r, c):
        qn = qn_ref[...]
        for j in range(0, step, HEAD_DIM):
            qb_ref[:, c + j:c + j + HEAD_DIM] = norm_rope(r[:, j:j + HEAD_DIM], qn, HEAD_DIM ** -0.5)

    def kb_epilogue(r, c):
        kn = kn_ref[...]
        for j in range(0, ATT_KV_DIM, HEAD_DIM):
            kb_ref[:, j:j + HEAD_DIM] = norm_rope(r[:, j:j + HEAD_DIM], kn, 1.0)

    def gt_epilogue(r, c):
        xs = r + dtb_ref[...]
        softplus = jnp.maximum(xs, 0.0) + jnp.log1p(jnp.exp(-jnp.abs(xs)))
        g = -jnp.exp(alog_ref[...]) * softplus
        pos = lax.broadcasted_iota(jnp.int32, (tm, LANES), 0) % CHUNK
        pre = g
        suf = g
        s = 1
        while s < CHUNK:
            pre = pre + jnp.where(pos >= s, pltpu.roll(pre, s, axis=0), 0.0)
            suf = suf + jnp.where(pos < CHUNK - s, pltpu.roll(suf, tm - s, axis=0), 0.0)
            s *= 2
        gt_ref[...] = jnp.where(lane < LA_HEADS, pre,
                                jnp.where(lane < 2 * LA_HEADS, suf, _sigmoid(r)))

    heavy = [(he, C_QKV + c, step, qkv_epilogue, c) for c in range(0, 3 * LA_DIM, step)]
    light = [(h, C_Z + c, step, z_epilogue, c) for c in range(0, LA_DIM, step)]
    light += [(h, C_GATE + c, step, gate_epilogue, c) for c in range(0, 2 * D_MODEL, step)]
    light += [(h, C_VB, ATT_KV_DIM, vb_epilogue, 0)]
    light += [(h, C_QB + c, step, qb_epilogue, c) for c in range(0, ATT_Q_DIM, step)]
    light += [(h, C_KB, ATT_KV_DIM, kb_epilogue, 0), (h, C_AB, LANES, gt_epilogue, 0)]
    stages = []
    per_heavy = -(-len(light) // len(heavy))
    for i, st in enumerate(heavy):
        stages += [st] + light[i * per_heavy:(i + 1) * per_heavy]
    assert len(stages) == len(heavy) + len(light)
    pending = None
    for lhs, c0, n, epilogue, c in stages:
        r = jnp.dot(lhs, w_ref[:, c0:c0 + n], preferred_element_type=F32)
        if pending is not None:
            pending[0](pending[1], pending[2])
        pending = (epilogue, r, c)
    pending[0](pending[1], pending[2])


def _proj(x2, nw, w, cw, bg, qn, kn, cos_t, sin_t, alog, dtb, T, tm):
    M = x2.shape[0]
    tpb = T // tm
    halo = 8
    hb = tm // halo
    row = lambda n: pl.BlockSpec((tm, n), lambda i: (i, 0))
    tab = pl.BlockSpec((tm, LANES), lambda i: (i % tpb, 0))
    prev = pl.BlockSpec((halo, D_MODEL), lambda i: (jnp.maximum(i * hb - 1, 0), 0))
    nxt = pl.BlockSpec((halo, D_MODEL), lambda i: (jnp.minimum((i + 1) * hb, M // halo - 1), 0))
    outs = [(3 * LA_DIM, BF16), (LA_DIM, F32), (ATT_Q_DIM, BF16), (ATT_KV_DIM, BF16),
            (ATT_KV_DIM, BF16), (2 * D_MODEL, F32), (LANES, F32)]
    return pl.pallas_call(
        functools.partial(_proj_kernel, tiles_per_seq=tpb),
        grid=(M // tm,),
        in_specs=[row(D_MODEL), prev, nxt, _resident((1, D_MODEL)), _resident((D_MODEL, N_PROJ)),
                  _resident((CONV_K, 3 * LA_DIM)),
                  _resident((1, 2 * D_MODEL)), _resident((1, HEAD_DIM)), _resident((1, HEAD_DIM)),
                  tab, tab, _resident((1, LANES)), _resident((1, LANES))],
        out_specs=[row(n) for n, _ in outs],
        out_shape=[jax.ShapeDtypeStruct((M, n), dt) for n, dt in outs],
        compiler_params=_params(1),
        name="proj",
    )(x2, x2, x2, nw, w, cw, bg, qn, kn, cos_t, sin_t, alog, dtb)


def _chunk_local(probs, hooks=()):
    C = CHUNK
    n = len(probs)
    assert 2 * C == LANES
    lane = lax.broadcasted_iota(jnp.int32, (C, LANES), 1)
    ii = lax.broadcasted_iota(jnp.int32, (C, LANES), 0)
    jj = lane % C
    hi = lane >= C
    eye_hi = jnp.where((ii == jj) & hi, 1.0, 0.0)
    incl = [(ii >= jj) if p["lower"] else (ii <= jj) for p in probs]
    strict = [(ii > jj) if p["lower"] else (ii < jj) for p in probs]
    col = lambda p, l: jnp.sum(jnp.where(lane == l, p["gtc"], 0.0), axis=1, keepdims=True)
    gc = [col(p, p["lane_g"]) for p in probs]
    beta = [col(p, p["lane_b"]) for p in probs]
    decay = [jnp.where(incl[i], jnp.exp(jnp.where(incl[i], gc[i] - probs[i]["gr"], 0.0)), 0.0)
             for i in range(n)]
    hooks = list(hooks)
    n_stages = 10
    done = [0, 0]

    def stage_done():
        done[0] += 1
        while done[1] < len(hooks) and (done[1] + 1) * n_stages <= done[0] * (len(hooks) + 1):
            hooks[done[1]]()
            done[1] += 1

    gram, f32 = {}, {}
    for p in probs:
        if p["cid"] not in gram:
            kk = jnp.concatenate([p["k"], p["k"]], axis=0)
            gram[p["cid"]] = lax.dot_general(
                jnp.concatenate([p["k"], p["q"]], axis=0), kk,
                (((1,), (1,)), ((), ())), preferred_element_type=F32)
            f32[p["cid"]] = tuple(p[name].astype(F32) for name in ("q", "k", "v"))
    stage_done()
    qf = [f32[p["cid"]][0] for p in probs]
    kf = [f32[p["cid"]][1] for p in probs]
    vf = [f32[p["cid"]][2] for p in probs]
    aq = [gram[p["cid"]] for p in probs]
    kb = [kf[i] * beta[i] for i in range(n)]
    neg_l = [jnp.where(strict[i], -(aq[i][:C] * beta[i] * decay[i]), 0.0) for i in range(n)]
    qk = [(aq[i][C:] * decay[i])[:, :C].astype(BF16) for i in range(n)]
    nb = [x.astype(BF16) for x in neg_l]
    sq = [jnp.dot(x[:, :C], x, preferred_element_type=F32) for x in nb]
    stage_done()
    z = [jnp.where(hi, eye_hi + neg_l[i], sq[i]) for i in range(n)]
    n_terms = 2
    while n_terms < C // 2:
        zb = [x.astype(BF16) for x in z]
        r = [jnp.dot(x[:, :C], x, preferred_element_type=F32) for x in zb]
        stage_done()
        z = [r[i] + jnp.where(hi, z[i], 0.0) for i in range(n)]
        n_terms *= 2
    zb = [x.astype(BF16) for x in z]
    pw = [x[:, :C] for x in zb]
    eg = [jnp.exp(x) for x in gc]
    rhs = [jnp.concatenate([vf[i] * beta[i], kb[i] * eg[i]], axis=1).astype(BF16)
           for i in range(n)]
    y = [jnp.dot(zb[i][:, C:], rhs[i], preferred_element_type=F32) for i in range(n)]
    stage_done()
    y = [(y[i] + jnp.dot(pw[i], y[i].astype(BF16), preferred_element_type=F32)).astype(BF16)
         for i in range(n)]
    stage_done()
    gl = [p["gr"][:, C - 1:C] if p["lower"] else p["gr"][:, 0:1] for p in probs]
    kd = [(kf[i] * jnp.exp(gl[i] - gc[i])).astype(BF16) for i in range(n)]
    r1 = [lax.dot_general(kd[i], y[i], (((0,), (0,)), ((), ())), preferred_element_type=F32)
          for i in range(n)]
    stage_done()
    r2 = [jnp.dot(qk[i], y[i], preferred_element_type=F32) for i in range(n)]
    stage_done()
    assert done == [n_stages, len(hooks)]
    out = []
    for i in range(n):
        qp = qf[i] * eg[i] - r2[i][:, HEAD_DIM:]
        kq = jnp.concatenate([r1[i][:, HEAD_DIM:], qp], axis=0).astype(BF16)
        out.append((kq, r1[i][:, :HEAD_DIM], r2[i][:, :HEAD_DIM], jnp.exp(gl[i])))
    return out


GDN_HP = 2
GDN_CPI = 4
KQ_ROWS = HEAD_DIM + CHUNK


def _gdn_kernel(q_ref, k_ref, v_ref, z_ref, gt_ref, gtr_ref, nw_ref,
                o_ref, kq_ref, bm_ref, oc_ref, dc_ref):
    T = q_ref.shape[0]
    N = T // CHUNK
    n_it = N // GDN_CPI
    n_chains = 2 * GDN_HP
    hd0 = pl.program_id(1) * GDN_HP

    def chain_chunk(ch, m):
        return m if ch % 2 == 0 else N - 1 - m

    def local_part(it, hooks):
        probs, dest = [], []
        for ch in range(n_chains):
            hp, d = divmod(ch, 2)
            hd = hd0 + hp
            sl = slice(hp * HEAD_DIM, (hp + 1) * HEAD_DIM)
            for j in range(GDN_CPI):
                c = chain_chunk(ch, it * GDN_CPI + j)
                tok = pl.ds(pl.multiple_of(c * CHUNK, CHUNK), CHUNK)
                probs.append(dict(cid=(ch, j), q=q_ref[tok, sl], k=k_ref[tok, sl], v=v_ref[tok, sl],
                                  gtc=gt_ref[tok, :],
                                  gr=gtr_ref[c, pl.ds(d * LA_HEADS + hd, 1), :],
                                  lane_g=d * LA_HEADS + hd, lane_b=(2 + d) * LA_HEADS + hd,
                                  lower=(d == 0)))
                dest.append((ch * N + c, ch * T + c * CHUNK))
        for (kq, bm, o0, dec), (slot, orow) in zip(_chunk_local(probs, hooks), dest):
            kq_ref[pl.ds(pl.multiple_of(slot * KQ_ROWS, CHUNK), KQ_ROWS), :] = kq
            bm_ref[pl.ds(pl.multiple_of(slot * HEAD_DIM, HEAD_DIM), HEAD_DIM), :] = bm
            oc_ref[pl.ds(pl.multiple_of(orow, CHUNK), CHUNK), :] = o0
            dc_ref[pl.ds(pl.multiple_of(slot * 8, 8), 8), :] = jnp.broadcast_to(dec, (8, LANES))

    def state_step(m, states):
        slots = [ch * N + chain_chunk(ch, m) for ch in range(n_chains)]
        orows = [pl.ds(pl.multiple_of(ch * T + chain_chunk(ch, m) * CHUNK, CHUNK), CHUNK)
                 for ch in range(n_chains)]
        r = [jnp.dot(kq_ref[pl.ds(pl.multiple_of(slots[ch] * KQ_ROWS, CHUNK), KQ_ROWS), :],
                     states[ch].astype(BF16), preferred_element_type=F32)
             for ch in range(n_chains)]
        new = []
        for ch in range(n_chains):
            dec = dc_ref[pl.ds(pl.multiple_of(slots[ch] * 8, 8), 8), :][0:1, :]
            bm = bm_ref[pl.ds(pl.multiple_of(slots[ch] * HEAD_DIM, HEAD_DIM), HEAD_DIM), :]
            new.append(states[ch] * dec + bm - r[ch][:HEAD_DIM])
            oc_ref[orows[ch], :] = oc_ref[orows[ch], :] + r[ch][HEAD_DIM:]
        return new

    local_part(0, ())

    def body(it, states):
        st = [list(states)]

        def hook(j):
            def run():
                st[0] = state_step((it - 1) * GDN_CPI + j, st[0])
            return run

        local_part(it, [hook(j) for j in range(GDN_CPI)])
        return tuple(st[0])

    zero = jnp.zeros((HEAD_DIM, HEAD_DIM), F32)
    states = list(lax.fori_loop(1, n_it, body, (zero,) * n_chains))
    for j in range(GDN_CPI):
        states = state_step((n_it - 1) * GDN_CPI + j, states)
    for hp in range(GDN_HP):
        sl = slice(hp * HEAD_DIM, (hp + 1) * HEAD_DIM)
        o = oc_ref[2 * hp * T:(2 * hp + 1) * T, :] + oc_ref[(2 * hp + 1) * T:(2 * hp + 2) * T, :]
        o_ref[:, sl] = (_rms(o, nw_ref[...]) * _silu(z_ref[:, sl])).astype(BF16)


def _gdn(qkv, z, gt, gtr, nw):
    B, T, _ = qkv.shape
    N = T // CHUNK
    hw = GDN_HP * HEAD_DIM
    nblk = LA_HEADS // GDN_HP
    seq = lambda off: pl.BlockSpec((None, T, hw), lambda b, h: (b, 0, off + h))
    chains = 2 * GDN_HP
    return pl.pallas_call(
        _gdn_kernel,
        grid=(B, nblk),
        in_specs=[seq(0), seq(nblk), seq(2 * nblk),
                  seq(0),
                  pl.BlockSpec((None, T, LANES), lambda b, h: (b, 0, 0)),
                  pl.BlockSpec((None, N, 4 * LA_HEADS, LANES), lambda b, h: (b, 0, 0, 0)),
                  pl.BlockSpec((1, HEAD_DIM), lambda b, h: (0, 0))],
        out_specs=seq(0),
        out_shape=jax.ShapeDtypeStruct((B, T, LA_DIM), BF16),
        scratch_shapes=[
            pltpu.VMEM((chains * N * KQ_ROWS, HEAD_DIM), BF16),
            pltpu.VMEM((chains * N * HEAD_DIM, HEAD_DIM), F32),
            pltpu.VMEM((chains * T, HEAD_DIM), F32),
            pltpu.VMEM((chains * N * 8, LANES), F32)],
        compiler_params=_params(2),
        name="gdn",
    )(qkv, qkv, qkv, z, gt, gtr, nw)


def _attn_kernel(q_ref, k_ref, v_ref, o_ref):
    k = k_ref[...]
    v = v_ref[...]
    for g in range(GQA_GROUP):
        sl = slice(g * HEAD_DIM, (g + 1) * HEAD_DIM)
        s = _mm_nt(q_ref[:, sl], k)
        p = jnp.exp(s - jnp.max(s, axis=-1, keepdims=True))
        l = jnp.sum(p, axis=-1, keepdims=True)
        o_ref[:, sl] = (_mm(p, v) / l).astype(BF16)


def _attn(qb, kb, vb, tq):
    B, T, _ = qb.shape
    gw = GQA_GROUP * HEAD_DIM
    return pl.pallas_call(
        _attn_kernel,
        grid=(B, KV_HEADS, T // tq),
        in_specs=[pl.BlockSpec((None, tq, gw), lambda b, kv, i: (b, i, kv)),
                  pl.BlockSpec((None, T, HEAD_DIM), lambda b, kv, i: (b, 0, kv)),
                  pl.BlockSpec((None, T, HEAD_DIM), lambda b, kv, i: (b, 0, kv))],
        out_specs=pl.BlockSpec((None, tq, gw), lambda b, kv, i: (b, i, kv)),
        out_shape=jax.ShapeDtypeStruct((B, T, ATT_Q_DIM), BF16),
        compiler_params=_params(3),
        name="attn",
    )(qb, kb, vb)


def _merge_kernel(x_ref, oa_ref, ob_ref, gate_ref, wa_ref, wb_ref, wo_ref, nw_ref, o_ref):
    ya = jnp.dot(oa_ref[...], wa_ref[...], preferred_element_type=F32)
    yb = jnp.dot(ob_ref[...], wb_ref[...], preferred_element_type=F32)
    mix = gate_ref[:, :D_MODEL] * ya + gate_ref[:, D_MODEL:] * yb
    mixed = jnp.dot(mix.astype(BF16), wo_ref[...], preferred_element_type=F32)
    o_ref[...] = x_ref[...] + _rms(mixed, nw_ref[...])


def _merge(x2, oa, ob, gate, wa, wb, wo, nw, tm):
    M = x2.shape[0]
    row = lambda n: pl.BlockSpec((tm, n), lambda i: (i, 0))
    sq = _resident((D_MODEL, D_MODEL))
    return pl.pallas_call(
        _merge_kernel,
        grid=(M // tm,),
        in_specs=[row(D_MODEL), row(D_MODEL), row(D_MODEL), row(2 * D_MODEL), sq, sq, sq,
                  _resident((1, D_MODEL))],
        out_specs=row(D_MODEL),
        out_shape=jax.ShapeDtypeStruct((M, D_MODEL), F32),
        compiler_params=_params(1),
        name="merge",
    )(x2, oa, ob, gate, wa, wb, wo, nw)


def _memkv_kernel(mem_ref, nw_ref, w_ref, o_ref):
    o_ref[...] = jnp.dot(_rms(mem_ref[...], nw_ref[...]).astype(BF16), w_ref[...],
                         preferred_element_type=F32).astype(BF16)


def _memkv(mem, nw, w):
    B, Mt, _ = mem.shape
    return pl.pallas_call(
        _memkv_kernel,
        grid=(B,),
        in_specs=[pl.BlockSpec((None, Mt, D_MODEL), lambda b: (b, 0, 0)),
                  _resident((1, D_MODEL)), _resident((D_MODEL, 2 * D_MODEL))],
        out_specs=pl.BlockSpec((None, Mt, 2 * D_MODEL), lambda b: (b, 0, 0)),
        out_shape=jax.ShapeDtypeStruct((B, Mt, 2 * D_MODEL), BF16),
        compiler_params=_params(1),
        name="memkv",
    )(mem, nw, w)


def _memattn_kernel(x_ref, kv_ref, prew_ref, wq_ref, wo_ref, postw_ref, o_ref, att_ref):
    x = x_ref[...]
    h = _rms(x, prew_ref[...]).astype(BF16)
    q = (jnp.dot(h, wq_ref[...], preferred_element_type=F32) * MEM_HEAD_DIM ** -0.5).astype(BF16)
    for hd in range(MEM_HEADS):
        sl = slice(hd * MEM_HEAD_DIM, (hd + 1) * MEM_HEAD_DIM)
        s = _mm_nt(q[:, sl], kv_ref[:, sl])
        p = jnp.exp(s - jnp.max(s, axis=-1, keepdims=True))
        l = jnp.sum(p, axis=-1, keepdims=True)
        v = kv_ref[:, D_MODEL + hd * MEM_HEAD_DIM:D_MODEL + (hd + 1) * MEM_HEAD_DIM]
        att_ref[:, sl] = (_mm(p, v) / l).astype(BF16)
    y = jnp.dot(att_ref[...], wo_ref[...], preferred_element_type=F32)
    o_ref[...] = x + _rms(y, postw_ref[...])


def _memattn(x2, kv, prew, wq, wo, postw, T, tm):
    M = x2.shape[0]
    Mt = kv.shape[1]
    tpb = T // tm
    row = pl.BlockSpec((tm, D_MODEL), lambda i: (i, 0))
    sq = _resident((D_MODEL, D_MODEL))
    vec = _resident((1, D_MODEL))
    return pl.pallas_call(
        _memattn_kernel,
        grid=(M // tm,),
        in_specs=[row, pl.BlockSpec((None, Mt, 2 * D_MODEL), lambda i: (i // tpb, 0, 0)),
                  vec, sq, sq, vec],
        out_specs=row,
        out_shape=jax.ShapeDtypeStruct((M, D_MODEL), F32),
        scratch_shapes=[pltpu.VMEM((tm, D_MODEL), BF16)],
        compiler_params=_params(1),
        name="memattn",
    )(x2, kv, prew, wq, wo, postw)


def _ffn_kernel(x_ref, prew_ref, wg_ref, wu_ref, wd_ref, postw_ref, o_ref, *, n_split):
    x = x_ref[...]
    h = _rms(x, prew_ref[...]).astype(BF16)
    d_ff = wg_ref.shape[1]
    cs = d_ff // n_split
    y = None
    for c in range(0, d_ff, cs):
        gate = jnp.dot(h, wg_ref[:, c:c + cs], preferred_element_type=F32)
        up = jnp.dot(h, wu_ref[:, c:c + cs], preferred_element_type=F32)
        part = jnp.dot((_silu(gate) * up).astype(BF16), wd_ref[c:c + cs, :],
                       preferred_element_type=F32)
        y = part if y is None else y + part
    o_ref[...] = x + _rms(y, postw_ref[...])


def _ffn(x2, prew, wg, wu, wd, postw, tm):
    M = x2.shape[0]
    d_ff = wg.shape[1]
    row = pl.BlockSpec((tm, D_MODEL), lambda i: (i, 0))
    vec = _resident((1, D_MODEL))
    return pl.pallas_call(
        functools.partial(_ffn_kernel, n_split=2),
        grid=(M // tm,),
        in_specs=[row, vec, _resident((D_MODEL, d_ff)), _resident((D_MODEL, d_ff)),
                  _resident((d_ff, D_MODEL)), vec],
        out_specs=row,
        out_shape=jax.ShapeDtypeStruct((M, D_MODEL), F32),
        compiler_params=_params(1),
        name="ffn",
    )(x2, prew, wg, wu, wd, postw)


def _rope_tables(T):
    rows = T // GRID_W
    row = jnp.repeat(jnp.arange(rows, dtype=F32), GRID_W)
    col = jnp.tile(jnp.arange(GRID_W, dtype=F32), rows)
    half = HEAD_DIM // 2
    inv_freq = ROPE_THETA ** (-jnp.arange(0, half, 2, dtype=F32) / half)
    ang_r = row[:, None] * inv_freq
    ang_c = col[:, None] * inv_freq
    cos_t = jnp.concatenate([jnp.cos(ang_r), jnp.cos(ang_r), jnp.cos(ang_c), jnp.cos(ang_c)], axis=-1)
    sin_t = jnp.concatenate([-jnp.sin(ang_r), jnp.sin(ang_r), -jnp.sin(ang_c), jnp.sin(ang_c)], axis=-1)
    return cos_t, sin_t


def _layer(x, mem, mix_pre_norm, w_in, conv_w, la_a_log, la_dt_bias, la_norm_w, w_out_a,
           q_norm_w, k_norm_w, w_out_b, b_gate, w_out, mix_post_norm,
           mem_pre_norm, mem_kv_norm, w_mq, w_mkv, w_mo, mem_post_norm,
           ffn_pre_norm, w_ffn_in, w_ffn_out, ffn_post_norm):
    B, T, D = x.shape
    assert D == D_MODEL and T % 256 == 0 and T % GRID_W == 0
    M = B * T
    N = T // CHUNK
    tm_proj = 256
    tm_tok = 512 if T % 512 == 0 else 256
    vec = lambda a: a.reshape(1, -1).astype(F32)

    a0 = 4 * LA_DIM
    b0 = a0 + 4 * LA_HEADS
    w_r = jnp.concatenate([w_in[:, :a0], w_in[:, b0:], w_in[:, a0:b0],
                           jnp.zeros((D, LANES - 4 * LA_HEADS), w_in.dtype)], axis=1).astype(BF16)
    pad16 = lambda a: jnp.concatenate([a.reshape(-1).astype(F32), jnp.zeros((LANES - 2 * LA_HEADS,), F32)]).reshape(1, LANES)
    cos_t, sin_t = _rope_tables(T)

    x2 = x.reshape(M, D)
    qkv, z, qb, kb, vb, gate, gt = _proj(x2, vec(mix_pre_norm), w_r, conv_w.astype(F32), vec(b_gate),
                                         vec(q_norm_w), vec(k_norm_w), cos_t, sin_t,
                                         pad16(la_a_log), pad16(la_dt_bias), T, tm_proj)
    gt3 = gt.reshape(B, T, LANES)
    gtr = gt3[:, :, :4 * LA_HEADS].reshape(B, N, CHUNK, 4 * LA_HEADS).transpose(0, 1, 3, 2)
    gtr = jnp.concatenate([gtr, gtr], axis=-1)
    oa = _gdn(qkv.reshape(B, T, 3 * LA_DIM), z.reshape(B, T, LA_DIM), gt3, gtr, vec(la_norm_w))
    ob = _attn(qb.reshape(B, T, ATT_Q_DIM), kb.reshape(B, T, ATT_KV_DIM),
               vb.reshape(B, T, ATT_KV_DIM), tm_tok)
    x2 = _merge(x2, oa.reshape(M, LA_DIM), ob.reshape(M, ATT_Q_DIM), gate, w_out_a.astype(BF16),
                w_out_b.astype(BF16), w_out.astype(BF16), vec(mix_post_norm), tm_tok)
    kv = _memkv(mem, vec(mem_kv_norm), w_mkv.astype(BF16))
    x2 = _memattn(x2, kv, vec(mem_pre_norm), w_mq.astype(BF16), w_mo.astype(BF16),
                  vec(mem_post_norm), T, tm_tok)
    d_ff = w_ffn_out.shape[0]
    x2 = _ffn(x2, vec(ffn_pre_norm), w_ffn_in[:, :d_ff].astype(BF16), w_ffn_in[:, d_ff:].astype(BF16),
              w_ffn_out.astype(BF16), vec(ffn_post_norm), tm_tok)
    return x2.reshape(B, T, D)


def kernel(x, mem, mix_pre_norm, w_in, conv_w, la_a_log, la_dt_bias, la_norm_w, w_out_a, q_norm_w, k_norm_w, w_out_b, b_gate, w_out, mix_post_norm, mem_pre_norm, mem_kv_norm, w_mq, w_mkv, w_mo, mem_post_norm, ffn_pre_norm, w_ffn_in, w_ffn_out, ffn_post_norm):
    depth = w_in.shape[0]
    for l in range(depth):
        x = _layer(x, mem, mix_pre_norm[l], w_in[l], conv_w[l], la_a_log[l], la_dt_bias[l],
                   la_norm_w[l], w_out_a[l], q_norm_w[l], k_norm_w[l], w_out_b[l], b_gate[l],
                   w_out[l], mix_post_norm[l], mem_pre_norm[l], mem_kv_norm[l], w_mq[l], w_mkv[l],
                   w_mo[l], mem_post_norm[l], ffn_pre_norm[l], w_ffn_in[l], w_ffn_out[l],
                   ffn_post_norm[l])
    return x
```

```python
import functools

import jax
import jax.numpy as jnp
from jax import lax
from jax.experimental import pallas as pl
from jax.experimental.pallas import tpu as pltpu

F32 = jnp.float32
BF16 = jnp.bfloat16

D_MODEL = 1024
HEAD_DIM = 128
LA_HEADS = 8
LA_DIM = LA_HEADS * HEAD_DIM
CONV_K = 5
CHUNK = 64
ATT_HEADS = 8
KV_HEADS = 2
GQA_GROUP = ATT_HEADS // KV_HEADS
ATT_Q_DIM = ATT_HEADS * HEAD_DIM
ATT_KV_DIM = KV_HEADS * HEAD_DIM
GRID_W = 64
ROPE_THETA = 10000.0
MEM_HEADS = 4
MEM_HEAD_DIM = D_MODEL // MEM_HEADS
EPS = 1e-6
LOG2_E = 1.4426950408889634

LANES = 128
VMEM_LIMIT = 56 * 1024 * 1024

C_QKV = 0
C_Z = C_QKV + 3 * LA_DIM
C_QB = C_Z + LA_DIM
C_KB = C_QB + ATT_Q_DIM
C_VB = C_KB + ATT_KV_DIM
C_GATE = C_VB + ATT_KV_DIM
C_AB = C_GATE + 2 * D_MODEL
N_PROJ = C_AB + LANES
PROJ_STEP = 256


def _mm(a, b):
    return jnp.dot(a.astype(BF16), b.astype(BF16), preferred_element_type=F32)


def _mm_nt(a, b):
    return lax.dot_general(a.astype(BF16), b.astype(BF16), (((1,), (1,)), ((), ())),
                           preferred_element_type=F32)


def _mm_tn(a, b):
    return lax.dot_general(a.astype(BF16), b.astype(BF16), (((0,), (0,)), ((), ())),
                           preferred_element_type=F32)


def _rms(x, w):
    return x * lax.rsqrt(jnp.mean(x * x, axis=-1, keepdims=True) + EPS) * w


def _sigmoid(x):
    return 1.0 / (1.0 + jnp.exp(-x))


def _silu(x):
    return x * _sigmoid(x)


def _resident(shape):
    nd = len(shape)
    return pl.BlockSpec(shape, lambda *_: (0,) * nd, pipeline_mode=pl.Buffered(1))


def _params(n_grid):
    return pltpu.CompilerParams(dimension_semantics=("arbitrary",) * n_grid,
                                vmem_limit_bytes=VMEM_LIMIT)


def _proj_kernel(x_ref, xp_ref, xn_ref, nw_ref, w_ref, cw_ref, bg_ref, qn_ref, kn_ref, cos_ref,
                 sin_ref, alog_ref, dtb_ref,
                 qkv_ref, z_ref, qb_ref, kb_ref, vb_ref, gate_ref, gt_ref, *, tiles_per_seq):
    tm = x_ref.shape[0]
    halo = xp_ref.shape[0]
    nw = nw_ref[...]
    h = _rms(x_ref[...], nw).astype(BF16)

    seq_tile = pl.program_id(0) % tiles_per_seq
    hp = jnp.where(seq_tile > 0, _rms(xp_ref[...], nw), 0.0).astype(BF16)
    hn = jnp.where(seq_tile < tiles_per_seq - 1, _rms(xn_ref[...], nw), 0.0).astype(BF16)
    he = jnp.concatenate([hp, h, hn], axis=0)
    pad = (CONV_K - 1) // 2
    step = PROJ_STEP
    cos = cos_ref[...]
    sin = sin_ref[...]
    lane = lax.broadcasted_iota(jnp.int32, (tm, LANES), 1)
    first = (lane % 64) < 32

    def norm_rope(y, nw, scale):
        y = _rms(y, nw)
        partner = jnp.where(first, pltpu.roll(y, 96, axis=1), pltpu.roll(y, 32, axis=1))
        return ((y * cos + partner * sin) * scale).astype(BF16)

    def qkv_epilogue(re, c):
        cw = cw_ref[:, c:c + step]
        acc = re[halo:halo + tm] * cw[pad:pad + 1, :]
        for j in range(CONV_K):
            s = j - pad
            if s != 0:
                shifted = pltpu.roll(re, (-s) % (tm + 2 * halo), axis=0)[halo:halo + tm]
                acc = acc + shifted * cw[j:j + 1, :]
        y = _silu(acc)
        for j in range(0, step, HEAD_DIM):
            col = c + j
            yh = y[:, j:j + HEAD_DIM]
            if col < 2 * LA_DIM:
                yh = yh * lax.rsqrt(jnp.sum(yh * yh, axis=-1, keepdims=True) + EPS)
            if col < LA_DIM:
                yh = yh * (HEAD_DIM ** -0.5)
            qkv_ref[:, col:col + HEAD_DIM] = yh.astype(BF16)

    def z_epilogue(r, c):
        z_ref[:, c:c + step] = r

    def gate_epilogue(r, c):
        gate_ref[:, c:c + step] = _sigmoid(r + bg_ref[:, c:c + step])

    def vb_epilogue(r, c):
        vb_ref[...] = r.astype(BF16)

    def qb_epilogue(r, c):
        qn = qn_ref[...]
        for j in range(0, step, HEAD_DIM):
            qb_ref[:, c + j:c + j + HEAD_DIM] = norm_rope(r[:, j:j + HEAD_DIM], qn,
                                                           HEAD_DIM ** -0.5 * LOG2_E)

    def kb_epilogue(r, c):
        kn = kn_ref[...]
        for j in range(0, ATT_KV_DIM, HEAD_DIM):
            kb_ref[:, j:j + HEAD_DIM] = norm_rope(r[:, j:j + HEAD_DIM], kn, 1.0)

    def gt_epilogue(r, c):
        xs = r + dtb_ref[...]
        softplus = jnp.maximum(xs, 0.0) + jnp.log1p(jnp.exp(-jnp.abs(xs)))
        g = -jnp.exp(alog_ref[...]) * softplus
        pos = lax.broadcasted_iota(jnp.int32, (tm, LANES), 0) % CHUNK
        pre = g
        suf = g
        s = 1
        while s < CHUNK:
            pre = pre + jnp.where(pos >= s, pltpu.roll(pre, s, axis=0), 0.0)
            suf = suf + jnp.where(pos < CHUNK - s, pltpu.roll(suf, tm - s, axis=0), 0.0)
            s *= 2
        gt_ref[...] = jnp.where(lane < LA_HEADS, pre,
                                jnp.where(lane < 2 * LA_HEADS, suf, _sigmoid(r)))

    def spread(few, many):
        out, j = [], 0
        for i, st in enumerate(many):
            while j < len(few) and j * len(many) <= i * len(few):
                out.append(few[j])
                j += 1
            out.append(st)
        assert j == len(few)
        return out

    conv = [(he, C_QKV + c, step, qkv_epilogue, c) for c in range(0, 3 * LA_DIM, step)]
    medium = [(h, C_QB + c, step, qb_epilogue, c) for c in range(0, ATT_Q_DIM, step)]
    medium += [(h, C_KB, ATT_KV_DIM, kb_epilogue, 0), (h, C_AB, LANES, gt_epilogue, 0)]
    cheap = [(h, C_GATE + c, step, gate_epilogue, c) for c in range(0, 2 * D_MODEL, step)]
    cheap += [(h, C_VB, ATT_KV_DIM, vb_epilogue, 0)]
    cheap += [(h, C_Z + c, step, z_epilogue, c) for c in range(0, LA_DIM, step)]
    stages = spread(conv, spread(medium, cheap))
    pending = None
    for lhs, c0, n, epilogue, c in stages:
        r = jnp.dot(lhs, w_ref[:, c0:c0 + n], preferred_element_type=F32)
        if pending is not None:
            pending[0](pending[1], pending[2])
        pending = (epilogue, r, c)
    pending[0](pending[1], pending[2])


def _proj(x2, nw, w, cw, bg, qn, kn, cos_t, sin_t, alog, dtb, T, tm):
    M = x2.shape[0]
    tpb = T // tm
    halo = 8
    hb = tm // halo
    row = lambda n: pl.BlockSpec((tm, n), lambda i: (i, 0))
    tab = pl.BlockSpec((tm, LANES), lambda i: (i % tpb, 0))
    prev = pl.BlockSpec((halo, D_MODEL), lambda i: (jnp.maximum(i * hb - 1, 0), 0))
    nxt = pl.BlockSpec((halo, D_MODEL), lambda i: (jnp.minimum((i + 1) * hb, M // halo - 1), 0))
    outs = [(3 * LA_DIM, BF16), (LA_DIM, F32), (ATT_Q_DIM, BF16), (ATT_KV_DIM, BF16),
            (ATT_KV_DIM, BF16), (2 * D_MODEL, F32), (LANES, F32)]
    return pl.pallas_call(
        functools.partial(_proj_kernel, tiles_per_seq=tpb),
        grid=(M // tm,),
        in_specs=[row(D_MODEL), prev, nxt, _resident((1, D_MODEL)), _resident((D_MODEL, N_PROJ)),
                  _resident((CONV_K, 3 * LA_DIM)),
                  _resident((1, 2 * D_MODEL)), _resident((1, HEAD_DIM)), _resident((1, HEAD_DIM)),
                  tab, tab, _resident((1, LANES)), _resident((1, LANES))],
        out_specs=[row(n) for n, _ in outs],
        out_shape=[jax.ShapeDtypeStruct((M, n), dt) for n, dt in outs],
        compiler_params=_params(1),
        name="proj",
    )(x2, x2, x2, nw, w, cw, bg, qn, kn, cos_t, sin_t, alog, dtb)


def _chunk_local(probs, hooks=()):
    C = CHUNK
    n = len(probs)
    assert 2 * C == LANES
    lane = lax.broadcasted_iota(jnp.int32, (C, LANES), 1)
    ii = lax.broadcasted_iota(jnp.int32, (C, LANES), 0)
    jj = lane % C
    hi = lane >= C
    eye_hi = jnp.where((ii == jj) & hi, 1.0, 0.0)
    incl = [(ii >= jj) if p["lower"] else (ii <= jj) for p in probs]
    strict = [(ii > jj) if p["lower"] else (ii < jj) for p in probs]
    col = lambda p, l: jnp.sum(jnp.where(lane == l, p["gtc"], 0.0), axis=1, keepdims=True)
    gc = [col(p, p["lane_g"]) for p in probs]
    beta = [col(p, p["lane_b"]) for p in probs]
    decay = [jnp.where(incl[i], jnp.exp(jnp.where(incl[i], gc[i] - probs[i]["gr"], 0.0)), 0.0)
             for i in range(n)]
    hooks = list(hooks)
    n_stages = 10
    done = [0, 0]

    def stage_done():
        done[0] += 1
        while done[1] < len(hooks) and (done[1] + 1) * n_stages <= done[0] * (len(hooks) + 1):
            hooks[done[1]]()
            done[1] += 1

    gram, f32 = {}, {}
    for p in probs:
        if p["cid"] not in gram:
            kk = jnp.concatenate([p["k"], p["k"]], axis=0)
            gram[p["cid"]] = lax.dot_general(
                jnp.concatenate([p["k"], p["q"]], axis=0), kk,
                (((1,), (1,)), ((), ())), preferred_element_type=F32)
            f32[p["cid"]] = tuple(p[name].astype(F32) for name in ("q", "k", "v"))
    stage_done()
    qf = [f32[p["cid"]][0] for p in probs]
    kf = [f32[p["cid"]][1] for p in probs]
    vf = [f32[p["cid"]][2] for p in probs]
    aq = [gram[p["cid"]] for p in probs]
    kb = [kf[i] * beta[i] for i in range(n)]
    neg_l = [jnp.where(strict[i], -(aq[i][:C] * beta[i] * decay[i]), 0.0) for i in range(n)]
    qk = [(aq[i][C:] * decay[i])[:, :C].astype(BF16) for i in range(n)]
    nb = [x.astype(BF16) for x in neg_l]
    sq = [jnp.dot(x[:, :C], x, preferred_element_type=F32) for x in nb]
    stage_done()
    z = [jnp.where(hi, eye_hi + neg_l[i], sq[i]) for i in range(n)]
    n_terms = 2
    while n_terms < C // 2:
        zb = [x.astype(BF16) for x in z]
        r = [jnp.dot(x[:, :C], x, preferred_element_type=F32) for x in zb]
        stage_done()
        z = [r[i] + jnp.where(hi, z[i], 0.0) for i in range(n)]
        n_terms *= 2
    zb = [x.astype(BF16) for x in z]
    pw = [x[:, :C] for x in zb]
    eg = [jnp.exp(x) for x in gc]
    rhs = [jnp.concatenate([vf[i] * beta[i], kb[i] * eg[i]], axis=1).astype(BF16)
           for i in range(n)]
    y = [jnp.dot(zb[i][:, C:], rhs[i], preferred_element_type=F32) for i in range(n)]
    stage_done()
    y = [(y[i] + jnp.dot(pw[i], y[i].astype(BF16), preferred_element_type=F32)).astype(BF16)
         for i in range(n)]
    stage_done()
    gl = [p["gr"][:, C - 1:C] if p["lower"] else p["gr"][:, 0:1] for p in probs]
    kd = [(kf[i] * jnp.exp(gl[i] - gc[i])).astype(BF16) for i in range(n)]
    r1 = [lax.dot_general(kd[i], y[i], (((0,), (0,)), ((), ())), preferred_element_type=F32)
          for i in range(n)]
    stage_done()
    r2 = [jnp.dot(qk[i], y[i], preferred_element_type=F32) for i in range(n)]
    stage_done()
    assert done == [n_stages, len(hooks)]
    out = []
    for i in range(n):
        qp = qf[i] * eg[i] - r2[i][:, HEAD_DIM:]
        kq = jnp.concatenate([r1[i][:, HEAD_DIM:], qp], axis=0).astype(BF16)
        out.append((kq, r1[i][:, :HEAD_DIM], r2[i][:, :HEAD_DIM], jnp.exp(gl[i])))
    return out


GDN_HP = 2
GDN_CPI = 4
KQ_ROWS = HEAD_DIM + CHUNK


def _gdn_kernel(q_ref, k_ref, v_ref, z_ref, gt_ref, gtr_ref, nw_ref,
                o_ref, kq_ref, bm_ref, oc_ref, dc_ref):
    T = q_ref.shape[0]
    N = T // CHUNK
    n_it = N // GDN_CPI
    n_chains = 2 * GDN_HP
    hd0 = pl.program_id(1) * GDN_HP

    def chain_chunk(ch, m):
        return m if ch % 2 == 0 else N - 1 - m

    def local_part(it, hooks):
        probs, dest = [], []
        for ch in range(n_chains):
            hp, d = divmod(ch, 2)
            hd = hd0 + hp
            sl = slice(hp * HEAD_DIM, (hp + 1) * HEAD_DIM)
            for j in range(GDN_CPI):
                c = chain_chunk(ch, it * GDN_CPI + j)
                tok = pl.ds(pl.multiple_of(c * CHUNK, CHUNK), CHUNK)
                probs.append(dict(cid=(ch, j), q=q_ref[tok, sl], k=k_ref[tok, sl], v=v_ref[tok, sl],
                                  gtc=gt_ref[tok, :],
                                  gr=gtr_ref[c, pl.ds(d * LA_HEADS + hd, 1), :],
                                  lane_g=d * LA_HEADS + hd, lane_b=(2 + d) * LA_HEADS + hd,
                                  lower=(d == 0)))
                dest.append((ch * N + c, ch * T + c * CHUNK))
        for (kq, bm, o0, dec), (slot, orow) in zip(_chunk_local(probs, hooks), dest):
            kq_ref[pl.ds(pl.multiple_of(slot * KQ_ROWS, CHUNK), KQ_ROWS), :] = kq
            bm_ref[pl.ds(pl.multiple_of(slot * HEAD_DIM, HEAD_DIM), HEAD_DIM), :] = bm
            oc_ref[pl.ds(pl.multiple_of(orow, CHUNK), CHUNK), :] = o0
            dc_ref[pl.ds(pl.multiple_of(slot * 8, 8), 8), :] = jnp.broadcast_to(dec, (8, LANES))

    def state_step(m, states):
        slots = [ch * N + chain_chunk(ch, m) for ch in range(n_chains)]
        orows = [pl.ds(pl.multiple_of(ch * T + chain_chunk(ch, m) * CHUNK, CHUNK), CHUNK)
                 for ch in range(n_chains)]
        r = [jnp.dot(kq_ref[pl.ds(pl.multiple_of(slots[ch] * KQ_ROWS, CHUNK), KQ_ROWS), :],
                     states[ch].astype(BF16), preferred_element_type=F32)
             for ch in range(n_chains)]
        new = []
        for ch in range(n_chains):
            dec = dc_ref[pl.ds(pl.multiple_of(slots[ch] * 8, 8), 8), :][0:1, :]
            bm = bm_ref[pl.ds(pl.multiple_of(slots[ch] * HEAD_DIM, HEAD_DIM), HEAD_DIM), :]
            new.append(states[ch] * dec + bm - r[ch][:HEAD_DIM])
            oc_ref[orows[ch], :] = oc_ref[orows[ch], :] + r[ch][HEAD_DIM:]
        return new

    local_part(0, ())

    def body(it, states):
        st = [list(states)]

        def hook(j):
            def run():
                st[0] = state_step((it - 1) * GDN_CPI + j, st[0])
            return run

        local_part(it, [hook(j) for j in range(GDN_CPI)])
        return tuple(st[0])

    zero = jnp.zeros((HEAD_DIM, HEAD_DIM), F32)
    states = list(lax.fori_loop(1, n_it, body, (zero,) * n_chains))
    for j in range(GDN_CPI):
        states = state_step((n_it - 1) * GDN_CPI + j, states)
    for hp in range(GDN_HP):
        sl = slice(hp * HEAD_DIM, (hp + 1) * HEAD_DIM)
        o = oc_ref[2 * hp * T:(2 * hp + 1) * T, :] + oc_ref[(2 * hp + 1) * T:(2 * hp + 2) * T, :]
        o_ref[:, sl] = (_rms(o, nw_ref[...]) * _silu(z_ref[:, sl])).astype(BF16)


def _gdn(qkv, z, gt, gtr, nw):
    B, T, _ = qkv.shape
    N = T // CHUNK
    hw = GDN_HP * HEAD_DIM
    nblk = LA_HEADS // GDN_HP
    seq = lambda off: pl.BlockSpec((None, T, hw), lambda b, h: (b, 0, off + h))
    chains = 2 * GDN_HP
    return pl.pallas_call(
        _gdn_kernel,
        grid=(B, nblk),
        in_specs=[seq(0), seq(nblk), seq(2 * nblk),
                  seq(0),
                  pl.BlockSpec((None, T, LANES), lambda b, h: (b, 0, 0)),
                  pl.BlockSpec((None, N, 4 * LA_HEADS, LANES), lambda b, h: (b, 0, 0, 0)),
                  pl.BlockSpec((1, HEAD_DIM), lambda b, h: (0, 0))],
        out_specs=seq(0),
        out_shape=jax.ShapeDtypeStruct((B, T, LA_DIM), BF16),
        scratch_shapes=[
            pltpu.VMEM((chains * N * KQ_ROWS, HEAD_DIM), BF16),
            pltpu.VMEM((chains * N * HEAD_DIM, HEAD_DIM), F32),
            pltpu.VMEM((chains * T, HEAD_DIM), F32),
            pltpu.VMEM((chains * N * 8, LANES), F32)],
        compiler_params=_params(2),
        name="gdn",
    )(qkv, qkv, qkv, z, gt, gtr, nw)


ATTN_KEY_BLOCK = 512


def _attn_kernel(q_ref, k_ref, v_ref, o_ref):
    k = k_ref[...]
    v = v_ref[...]
    heads = [slice(g * HEAD_DIM, (g + 1) * HEAD_DIM) for g in range(GQA_GROUP)]
    scores = lambda sl: _mm_nt(q_ref[:, sl], k)
    s_next = scores(heads[0])
    for g, sl in enumerate(heads):
        s = s_next
        if g + 1 < len(heads):
            s_next = scores(heads[g + 1])
        m = jnp.max(s, axis=-1, keepdims=True)
        o = l = None
        for t0 in range(0, k.shape[0], ATTN_KEY_BLOCK):
            p = jnp.exp2(s[:, t0:t0 + ATTN_KEY_BLOCK] - m)
            lb = jnp.sum(p, axis=-1, keepdims=True)
            ob = _mm(p, v[t0:t0 + ATTN_KEY_BLOCK])
            o, l = (ob, lb) if o is None else (o + ob, l + lb)
        o_ref[:, sl] = (o / l).astype(BF16)


def _attn(qb, kb, vb, tq):
    B, T, _ = qb.shape
    gw = GQA_GROUP * HEAD_DIM
    return pl.pallas_call(
        _attn_kernel,
        grid=(B, KV_HEADS, T // tq),
        in_specs=[pl.BlockSpec((None, tq, gw), lambda b, kv, i: (b, i, kv)),
                  pl.BlockSpec((None, T, HEAD_DIM), lambda b, kv, i: (b, 0, kv)),
                  pl.BlockSpec((None, T, HEAD_DIM), lambda b, kv, i: (b, 0, kv))],
        out_specs=pl.BlockSpec((None, tq, gw), lambda b, kv, i: (b, i, kv)),
        out_shape=jax.ShapeDtypeStruct((B, T, ATT_Q_DIM), BF16),
        compiler_params=_params(3),
        name="attn",
    )(qb, kb, vb)


def _merge_kernel(x_ref, oa_ref, ob_ref, gate_ref, wa_ref, wb_ref, wo_ref, nw_ref, o_ref):
    ya = jnp.dot(oa_ref[...], wa_ref[...], preferred_element_type=F32)
    yb = jnp.dot(ob_ref[...], wb_ref[...], preferred_element_type=F32)
    mix = gate_ref[:, :D_MODEL] * ya + gate_ref[:, D_MODEL:] * yb
    mixed = jnp.dot(mix.astype(BF16), wo_ref[...], preferred_element_type=F32)
    o_ref[...] = x_ref[...] + _rms(mixed, nw_ref[...])


def _merge(x2, oa, ob, gate, wa, wb, wo, nw, tm):
    M = x2.shape[0]
    row = lambda n: pl.BlockSpec((tm, n), lambda i: (i, 0))
    sq = _resident((D_MODEL, D_MODEL))
    return pl.pallas_call(
        _merge_kernel,
        grid=(M // tm,),
        in_specs=[row(D_MODEL), row(D_MODEL), row(D_MODEL), row(2 * D_MODEL), sq, sq, sq,
                  _resident((1, D_MODEL))],
        out_specs=row(D_MODEL),
        out_shape=jax.ShapeDtypeStruct((M, D_MODEL), F32),
        compiler_params=_params(1),
        name="merge",
    )(x2, oa, ob, gate, wa, wb, wo, nw)


def _memkv_kernel(mem_ref, nw_ref, w_ref, o_ref):
    o_ref[...] = jnp.dot(_rms(mem_ref[...], nw_ref[...]).astype(BF16), w_ref[...],
                         preferred_element_type=F32).astype(BF16)


def _memkv(mem, nw, w):
    B, Mt, _ = mem.shape
    return pl.pallas_call(
        _memkv_kernel,
        grid=(B,),
        in_specs=[pl.BlockSpec((None, Mt, D_MODEL), lambda b: (b, 0, 0)),
                  _resident((1, D_MODEL)), _resident((D_MODEL, 2 * D_MODEL))],
        out_specs=pl.BlockSpec((None, Mt, 2 * D_MODEL), lambda b: (b, 0, 0)),
        out_shape=jax.ShapeDtypeStruct((B, Mt, 2 * D_MODEL), BF16),
        compiler_params=_params(1),
        name="memkv",
    )(mem, nw, w)


def _memattn_kernel(x_ref, kv_ref, prew_ref, wq_ref, wo_ref, postw_ref, o_ref, att_ref):
    x = x_ref[...]
    h = _rms(x, prew_ref[...]).astype(BF16)
    q = (jnp.dot(h, wq_ref[...], preferred_element_type=F32)
         * (MEM_HEAD_DIM ** -0.5 * LOG2_E)).astype(BF16)
    heads = [slice(hd * MEM_HEAD_DIM, (hd + 1) * MEM_HEAD_DIM) for hd in range(MEM_HEADS)]
    scores = lambda sl: _mm_nt(q[:, sl], kv_ref[:, sl])
    s_next = scores(heads[0])
    for hd, sl in enumerate(heads):
        s = s_next
        if hd + 1 < MEM_HEADS:
            s_next = scores(heads[hd + 1])
        p = jnp.exp2(s - jnp.max(s, axis=-1, keepdims=True))
        l = jnp.sum(p, axis=-1, keepdims=True)
        v = kv_ref[:, D_MODEL + hd * MEM_HEAD_DIM:D_MODEL + (hd + 1) * MEM_HEAD_DIM]
        att_ref[:, sl] = (_mm(p, v) / l).astype(BF16)
    y = jnp.dot(att_ref[...], wo_ref[...], preferred_element_type=F32)
    o_ref[...] = x + _rms(y, postw_ref[...])


def _memattn(x2, kv, prew, wq, wo, postw, T, tm):
    M = x2.shape[0]
    Mt = kv.shape[1]
    tpb = T // tm
    row = pl.BlockSpec((tm, D_MODEL), lambda i: (i, 0))
    sq = _resident((D_MODEL, D_MODEL))
    vec = _resident((1, D_MODEL))
    return pl.pallas_call(
        _memattn_kernel,
        grid=(M // tm,),
        in_specs=[row, pl.BlockSpec((None, Mt, 2 * D_MODEL), lambda i: (i // tpb, 0, 0)),
                  vec, sq, sq, vec],
        out_specs=row,
        out_shape=jax.ShapeDtypeStruct((M, D_MODEL), F32),
        scratch_shapes=[pltpu.VMEM((tm, D_MODEL), BF16)],
        compiler_params=_params(1),
        name="memattn",
    )(x2, kv, prew, wq, wo, postw)


def _ffn_kernel(x_ref, prew_ref, wg_ref, wu_ref, wd_ref, postw_ref, o_ref, *, n_split):
    x = x_ref[...]
    h = _rms(x, prew_ref[...]).astype(BF16)
    d_ff = wg_ref.shape[1]
    cs = d_ff // n_split
    assert cs % LANES == 0 and cs * n_split == d_ff
    y = None
    for c in range(0, d_ff, cs):
        gate = jnp.dot(h, wg_ref[:, c:c + cs], preferred_element_type=F32)
        up = jnp.dot(h, wu_ref[:, c:c + cs], preferred_element_type=F32)
        part = jnp.dot((_silu(gate) * up).astype(BF16), wd_ref[c:c + cs, :],
                       preferred_element_type=F32)
        y = part if y is None else y + part
    o_ref[...] = x + _rms(y, postw_ref[...])


def _ffn(x2, prew, wg, wu, wd, postw, tm):
    M = x2.shape[0]
    d_ff = wg.shape[1]
    row = pl.BlockSpec((tm, D_MODEL), lambda i: (i, 0))
    vec = _resident((1, D_MODEL))
    return pl.pallas_call(
        functools.partial(_ffn_kernel, n_split=11),
        grid=(M // tm,),
        in_specs=[row, vec, _resident((D_MODEL, d_ff)), _resident((D_MODEL, d_ff)),
                  _resident((d_ff, D_MODEL)), vec],
        out_specs=row,
        out_shape=jax.ShapeDtypeStruct((M, D_MODEL), F32),
        compiler_params=_params(1),
        name="ffn",
    )(x2, prew, wg, wu, wd, postw)


def _rope_tables(T):
    rows = T // GRID_W
    row = jnp.repeat(jnp.arange(rows, dtype=F32), GRID_W)
    col = jnp.tile(jnp.arange(GRID_W, dtype=F32), rows)
    half = HEAD_DIM // 2
    inv_freq = ROPE_THETA ** (-jnp.arange(0, half, 2, dtype=F32) / half)
    ang_r = row[:, None] * inv_freq
    ang_c = col[:, None] * inv_freq
    cos_t = jnp.concatenate([jnp.cos(ang_r), jnp.cos(ang_r), jnp.cos(ang_c), jnp.cos(ang_c)], axis=-1)
    sin_t = jnp.concatenate([-jnp.sin(ang_r), jnp.sin(ang_r), -jnp.sin(ang_c), jnp.sin(ang_c)], axis=-1)
    return cos_t, sin_t


def _layer(x, mem, mix_pre_norm, w_in, conv_w, la_a_log, la_dt_bias, la_norm_w, w_out_a,
           q_norm_w, k_norm_w, w_out_b, b_gate, w_out, mix_post_norm,
           mem_pre_norm, mem_kv_norm, w_mq, w_mkv, w_mo, mem_post_norm,
           ffn_pre_norm, w_ffn_in, w_ffn_out, ffn_post_norm):
    B, T, D = x.shape
    assert D == D_MODEL and T % 256 == 0 and T % GRID_W == 0
    M = B * T
    N = T // CHUNK
    tm_proj = 256
    tm_tok = 512 if T % 512 == 0 else 256
    vec = lambda a: a.reshape(1, -1).astype(F32)

    a0 = 4 * LA_DIM
    b0 = a0 + 4 * LA_HEADS
    w_r = jnp.concatenate([w_in[:, :a0], w_in[:, b0:], w_in[:, a0:b0],
                           jnp.zeros((D, LANES - 4 * LA_HEADS), w_in.dtype)], axis=1).astype(BF16)
    pad16 = lambda a: jnp.concatenate([a.reshape(-1).astype(F32), jnp.zeros((LANES - 2 * LA_HEADS,), F32)]).reshape(1, LANES)
    cos_t, sin_t = _rope_tables(T)

    x2 = x.reshape(M, D)
    qkv, z, qb, kb, vb, gate, gt = _proj(x2, vec(mix_pre_norm), w_r, conv_w.astype(F32), vec(b_gate),
                                         vec(q_norm_w), vec(k_norm_w), cos_t, sin_t,
                                         pad16(la_a_log), pad16(la_dt_bias), T, tm_proj)
    gt3 = gt.reshape(B, T, LANES)
    gtr = gt3[:, :, :4 * LA_HEADS].reshape(B, N, CHUNK, 4 * LA_HEADS).transpose(0, 1, 3, 2)
    gtr = jnp.concatenate([gtr, gtr], axis=-1)
    oa = _gdn(qkv.reshape(B, T, 3 * LA_DIM), z.reshape(B, T, LA_DIM), gt3, gtr, vec(la_norm_w))
    ob = _attn(qb.reshape(B, T, ATT_Q_DIM), kb.reshape(B, T, ATT_KV_DIM),
               vb.reshape(B, T, ATT_KV_DIM), tm_tok)
    x2 = _merge(x2, oa.reshape(M, LA_DIM), ob.reshape(M, ATT_Q_DIM), gate, w_out_a.astype(BF16),
                w_out_b.astype(BF16), w_out.astype(BF16), vec(mix_post_norm), tm_tok)
    kv = _memkv(mem, vec(mem_kv_norm), w_mkv.astype(BF16))
    x2 = _memattn(x2, kv, vec(mem_pre_norm), w_mq.astype(BF16), w_mo.astype(BF16),
                  vec(mem_post_norm), T, tm_tok)
    d_ff = w_ffn_out.shape[0]
    x2 = _ffn(x2, vec(ffn_pre_norm), w_ffn_in[:, :d_ff].astype(BF16), w_ffn_in[:, d_ff:].astype(BF16),
              w_ffn_out.astype(BF16), vec(ffn_post_norm), tm_tok)
    return x2.reshape(B, T, D)


def kernel(x, mem, mix_pre_norm, w_in, conv_w, la_a_log, la_dt_bias, la_norm_w, w_out_a, q_norm_w, k_norm_w, w_out_b, b_gate, w_out, mix_post_norm, mem_pre_norm, mem_kv_norm, w_mq, w_mkv, w_mo, mem_post_norm, ffn_pre_norm, w_ffn_in, w_ffn_out, ffn_post_norm):
    depth = w_in.shape[0]
    for l in range(depth):
        x = _layer(x, mem, mix_pre_norm[l], w_in[l], conv_w[l], la_a_log[l], la_dt_bias[l],
                   la_norm_w[l], w_out_a[l], q_norm_w[l], k_norm_w[l], w_out_b[l], b_gate[l],
                   w_out[l], mix_post_norm[l], mem_pre_norm[l], mem_kv_norm[l], w_mq[l], w_mkv[l],
                   w_mo[l], mem_post_norm[l], ffn_pre_norm[l], w_ffn_in[l], w_ffn_out[l],
                   ffn_post_norm[l])
    return x
```

```python
import functools

import jax
import jax.numpy as jnp
from jax import lax
from jax.experimental import pallas as pl
from jax.experimental.pallas import tpu as pltpu

F32 = jnp.float32
BF16 = jnp.bfloat16

D_MODEL = 1024
HEAD_DIM = 128
LA_HEADS = 8
LA_DIM = LA_HEADS * HEAD_DIM
CONV_K = 5
CHUNK = 64
ATT_HEADS = 8
KV_HEADS = 2
GQA_GROUP = ATT_HEADS // KV_HEADS
ATT_Q_DIM = ATT_HEADS * HEAD_DIM
ATT_KV_DIM = KV_HEADS * HEAD_DIM
GRID_W = 64
ROPE_THETA = 10000.0
MEM_HEADS = 4
MEM_HEAD_DIM = D_MODEL // MEM_HEADS
EPS = 1e-6
LOG2_E = 1.4426950408889634

LANES = 128
VMEM_LIMIT = 56 * 1024 * 1024

C_QKV = 0
C_Z = C_QKV + 3 * LA_DIM
C_QB = C_Z + LA_DIM
C_KB = C_QB + ATT_Q_DIM
C_VB = C_KB + ATT_KV_DIM
C_GATE = C_VB + ATT_KV_DIM
C_AB = C_GATE + 2 * D_MODEL
N_PROJ = C_AB + LANES
PROJ_STEP = 256


def _mm(a, b):
    return jnp.dot(a.astype(BF16), b.astype(BF16), preferred_element_type=F32)


def _mm_nt(a, b):
    return lax.dot_general(a.astype(BF16), b.astype(BF16), (((1,), (1,)), ((), ())),
                           preferred_element_type=F32)


def _mm_tn(a, b):
    return lax.dot_general(a.astype(BF16), b.astype(BF16), (((0,), (0,)), ((), ())),
                           preferred_element_type=F32)


def _rms(x, w):
    return x * lax.rsqrt(jnp.mean(x * x, axis=-1, keepdims=True) + EPS) * w


def _sigmoid(x):
    return 1.0 / (1.0 + jnp.exp(-x))


def _silu(x):
    return x * _sigmoid(x)


def _resident(shape):
    nd = len(shape)
    return pl.BlockSpec(shape, lambda *_: (0,) * nd, pipeline_mode=pl.Buffered(1))


def _params(n_grid):
    return pltpu.CompilerParams(dimension_semantics=("arbitrary",) * n_grid,
                                vmem_limit_bytes=VMEM_LIMIT)


def _proj_kernel(x_ref, xp_ref, xn_ref, nw_ref, w_ref, cw_ref, bg_ref, qn_ref, kn_ref, cos_ref,
                 sin_ref, alog_ref, dtb_ref,
                 qkv_ref, z_ref, qb_ref, kb_ref, vb_ref, gate_ref, gt_ref, *, tiles_per_seq):
    tm = x_ref.shape[0]
    halo = xp_ref.shape[0]
    nw = nw_ref[...]
    h = _rms(x_ref[...], nw).astype(BF16)

    seq_tile = pl.program_id(0) % tiles_per_seq
    hp = jnp.where(seq_tile > 0, _rms(xp_ref[...], nw), 0.0).astype(BF16)
    hn = jnp.where(seq_tile < tiles_per_seq - 1, _rms(xn_ref[...], nw), 0.0).astype(BF16)
    he = jnp.concatenate([hp, h, hn], axis=0)
    pad = (CONV_K - 1) // 2
    step = PROJ_STEP
    cos = cos_ref[...]
    sin = sin_ref[...]
    lane = lax.broadcasted_iota(jnp.int32, (tm, LANES), 1)
    first = (lane % 64) < 32

    def norm_rope(y, nw, scale):
        y = _rms(y, nw)
        partner = jnp.where(first, pltpu.roll(y, 96, axis=1), pltpu.roll(y, 32, axis=1))
        return ((y * cos + partner * sin) * scale).astype(BF16)

    def qkv_epilogue(re, c):
        cw = cw_ref[:, c:c + step]
        acc = re[halo:halo + tm] * cw[pad:pad + 1, :]
        for j in range(CONV_K):
            s = j - pad
            if s != 0:
                shifted = pltpu.roll(re, (-s) % (tm + 2 * halo), axis=0)[halo:halo + tm]
                acc = acc + shifted * cw[j:j + 1, :]
        y = _silu(acc)
        for j in range(0, step, HEAD_DIM):
            col = c + j
            yh = y[:, j:j + HEAD_DIM]
            if col < 2 * LA_DIM:
                yh = yh * lax.rsqrt(jnp.sum(yh * yh, axis=-1, keepdims=True) + EPS)
            if col < LA_DIM:
                yh = yh * (HEAD_DIM ** -0.5)
            qkv_ref[:, col:col + HEAD_DIM] = yh.astype(BF16)

    def z_epilogue(r, c):
        z_ref[:, c:c + step] = r

    def gate_epilogue(r, c):
        gate_ref[:, c:c + step] = _sigmoid(r + bg_ref[:, c:c + step])

    def vb_epilogue(r, c):
        vb_ref[...] = r.astype(BF16)

    def qb_epilogue(r, c):
        qn = qn_ref[...]
        for j in range(0, step, HEAD_DIM):
            qb_ref[:, c + j:c + j + HEAD_DIM] = norm_rope(r[:, j:j + HEAD_DIM], qn,
                                                           HEAD_DIM ** -0.5 * LOG2_E)

    def kb_epilogue(r, c):
        kn = kn_ref[...]
        for j in range(0, ATT_KV_DIM, HEAD_DIM):
            kb_ref[:, j:j + HEAD_DIM] = norm_rope(r[:, j:j + HEAD_DIM], kn, 1.0)

    def gt_epilogue(r, c):
        xs = r + dtb_ref[...]
        softplus = jnp.maximum(xs, 0.0) + jnp.log1p(jnp.exp(-jnp.abs(xs)))
        g = -jnp.exp(alog_ref[...]) * softplus
        pos = lax.broadcasted_iota(jnp.int32, (tm, LANES), 0) % CHUNK
        pre = g
        suf = g
        s = 1
        while s < CHUNK:
            pre = pre + jnp.where(pos >= s, pltpu.roll(pre, s, axis=0), 0.0)
            suf = suf + jnp.where(pos < CHUNK - s, pltpu.roll(suf, tm - s, axis=0), 0.0)
            s *= 2
        gt_ref[...] = jnp.where(lane < LA_HEADS, pre,
                                jnp.where(lane < 2 * LA_HEADS, suf, _sigmoid(r)))

    def spread(few, many):
        out, j = [], 0
        for i, st in enumerate(many):
            while j < len(few) and j * len(many) <= i * len(few):
                out.append(few[j])
                j += 1
            out.append(st)
        assert j == len(few)
        return out

    conv = [(he, C_QKV + c, step, qkv_epilogue, c) for c in range(0, 3 * LA_DIM, step)]
    medium = [(h, C_QB + c, step, qb_epilogue, c) for c in range(0, ATT_Q_DIM, step)]
    medium += [(h, C_KB, ATT_KV_DIM, kb_epilogue, 0), (h, C_AB, LANES, gt_epilogue, 0)]
    cheap = [(h, C_GATE + c, step, gate_epilogue, c) for c in range(0, 2 * D_MODEL, step)]
    cheap += [(h, C_VB, ATT_KV_DIM, vb_epilogue, 0)]
    cheap += [(h, C_Z + c, step, z_epilogue, c) for c in range(0, LA_DIM, step)]
    stages = spread(conv, spread(medium, cheap))
    pending = None
    for lhs, c0, n, epilogue, c in stages:
        r = jnp.dot(lhs, w_ref[:, c0:c0 + n], preferred_element_type=F32)
        if pending is not None:
            pending[0](pending[1], pending[2])
        pending = (epilogue, r, c)
    pending[0](pending[1], pending[2])


def _proj(x2, nw, w, cw, bg, qn, kn, cos_t, sin_t, alog, dtb, T, tm):
    M = x2.shape[0]
    tpb = T // tm
    halo = 8
    hb = tm // halo
    row = lambda n: pl.BlockSpec((tm, n), lambda i: (i, 0))
    tab = pl.BlockSpec((tm, LANES), lambda i: (i % tpb, 0))
    prev = pl.BlockSpec((halo, D_MODEL), lambda i: (jnp.maximum(i * hb - 1, 0), 0))
    nxt = pl.BlockSpec((halo, D_MODEL), lambda i: (jnp.minimum((i + 1) * hb, M // halo - 1), 0))
    outs = [(3 * LA_DIM, BF16), (LA_DIM, F32), (ATT_Q_DIM, BF16), (ATT_KV_DIM, BF16),
            (ATT_KV_DIM, BF16), (2 * D_MODEL, F32), (LANES, F32)]
    return pl.pallas_call(
        functools.partial(_proj_kernel, tiles_per_seq=tpb),
        grid=(M // tm,),
        in_specs=[row(D_MODEL), prev, nxt, _resident((1, D_MODEL)), _resident((D_MODEL, N_PROJ)),
                  _resident((CONV_K, 3 * LA_DIM)),
                  _resident((1, 2 * D_MODEL)), _resident((1, HEAD_DIM)), _resident((1, HEAD_DIM)),
                  tab, tab, _resident((1, LANES)), _resident((1, LANES))],
        out_specs=[row(n) for n, _ in outs],
        out_shape=[jax.ShapeDtypeStruct((M, n), dt) for n, dt in outs],
        compiler_params=_params(1),
        name="proj",
    )(x2, x2, x2, nw, w, cw, bg, qn, kn, cos_t, sin_t, alog, dtb)


def _chunk_local(probs, hooks=()):
    C = CHUNK
    n = len(probs)
    assert 2 * C == LANES
    lane = lax.broadcasted_iota(jnp.int32, (C, LANES), 1)
    ii = lax.broadcasted_iota(jnp.int32, (C, LANES), 0)
    jj = lane % C
    hi = lane >= C
    eye_hi = jnp.where((ii == jj) & hi, 1.0, 0.0)
    incl = [(ii >= jj) if p["lower"] else (ii <= jj) for p in probs]
    strict = [(ii > jj) if p["lower"] else (ii < jj) for p in probs]
    col = lambda p, l: jnp.sum(jnp.where(lane == l, p["gtc"], 0.0), axis=1, keepdims=True)
    gc = [col(p, p["lane_g"]) for p in probs]
    beta = [col(p, p["lane_b"]) for p in probs]
    decay = [jnp.where(incl[i], jnp.exp(jnp.where(incl[i], gc[i] - probs[i]["gr"], 0.0)), 0.0)
             for i in range(n)]
    hooks = list(hooks)
    n_stages = 10
    done = [0, 0]

    def stage_done():
        done[0] += 1
        while done[1] < len(hooks) and (done[1] + 1) * n_stages <= done[0] * (len(hooks) + 1):
            hooks[done[1]]()
            done[1] += 1

    gram, f32 = {}, {}
    for p in probs:
        if p["cid"] not in gram:
            kk = jnp.concatenate([p["k"], p["k"]], axis=0)
            gram[p["cid"]] = lax.dot_general(
                jnp.concatenate([p["k"], p["q"]], axis=0), kk,
                (((1,), (1,)), ((), ())), preferred_element_type=F32)
            f32[p["cid"]] = tuple(p[name].astype(F32) for name in ("q", "k", "v"))
    stage_done()
    qf = [f32[p["cid"]][0] for p in probs]
    kf = [f32[p["cid"]][1] for p in probs]
    vf = [f32[p["cid"]][2] for p in probs]
    aq = [gram[p["cid"]] for p in probs]
    kb = [kf[i] * beta[i] for i in range(n)]
    neg_l = [jnp.where(strict[i], -(aq[i][:C] * beta[i] * decay[i]), 0.0) for i in range(n)]
    qk = [(aq[i][C:] * decay[i])[:, :C].astype(BF16) for i in range(n)]
    nb = [x.astype(BF16) for x in neg_l]
    sq = [jnp.dot(x[:, :C], x, preferred_element_type=F32) for x in nb]
    stage_done()
    z = [jnp.where(hi, eye_hi + neg_l[i], sq[i]) for i in range(n)]
    n_terms = 2
    while n_terms < C // 2:
        zb = [x.astype(BF16) for x in z]
        r = [jnp.dot(x[:, :C], x, preferred_element_type=F32) for x in zb]
        stage_done()
        z = [r[i] + jnp.where(hi, z[i], 0.0) for i in range(n)]
        n_terms *= 2
    zb = [x.astype(BF16) for x in z]
    pw = [x[:, :C] for x in zb]
    eg = [jnp.exp(x) for x in gc]
    rhs = [jnp.concatenate([vf[i] * beta[i], kb[i] * eg[i]], axis=1).astype(BF16)
           for i in range(n)]
    y = [jnp.dot(zb[i][:, C:], rhs[i], preferred_element_type=F32) for i in range(n)]
    stage_done()
    y = [(y[i] + jnp.dot(pw[i], y[i].astype(BF16), preferred_element_type=F32)).astype(BF16)
         for i in range(n)]
    stage_done()
    gl = [p["gr"][:, C - 1:C] if p["lower"] else p["gr"][:, 0:1] for p in probs]
    kd = [(kf[i] * jnp.exp(gl[i] - gc[i])).astype(BF16) for i in range(n)]
    r1 = [lax.dot_general(kd[i], y[i], (((0,), (0,)), ((), ())), preferred_element_type=F32)
          for i in range(n)]
    stage_done()
    r2 = [jnp.dot(qk[i], y[i], preferred_element_type=F32) for i in range(n)]
    stage_done()
    assert done == [n_stages, len(hooks)]
    out = []
    for i in range(n):
        qp = qf[i] * eg[i] - r2[i][:, HEAD_DIM:]
        kq = jnp.concatenate([r1[i][:, HEAD_DIM:], qp], axis=0).astype(BF16)
        out.append((kq, r1[i][:, :HEAD_DIM], r2[i][:, :HEAD_DIM], jnp.exp(gl[i])))
    return out


GDN_HP = 2
GDN_CPI = 4
KQ_ROWS = HEAD_DIM + CHUNK


ATTN_KEY_BLOCK = 512


def _mixers_kernel(q_ref, k_ref, v_ref, z_ref, gt_ref, gtr_ref, nw_ref, qa_ref, ka_ref, va_ref,
                   o_ref, oa_ref, kq_ref, bm_ref, oc_ref, dc_ref):
    T = q_ref.shape[0]
    N = T // CHUNK
    n_it = N // GDN_CPI
    n_chains = 2 * GDN_HP
    hd0 = pl.program_id(1) * GDN_HP

    def chain_chunk(ch, m):
        return m if ch % 2 == 0 else N - 1 - m

    def local_part(it, hooks):
        probs, dest = [], []
        for ch in range(n_chains):
            hp, d = divmod(ch, 2)
            hd = hd0 + hp
            sl = slice(hp * HEAD_DIM, (hp + 1) * HEAD_DIM)
            for j in range(GDN_CPI):
                c = chain_chunk(ch, it * GDN_CPI + j)
                tok = pl.ds(pl.multiple_of(c * CHUNK, CHUNK), CHUNK)
                probs.append(dict(cid=(ch, j), q=q_ref[tok, sl], k=k_ref[tok, sl], v=v_ref[tok, sl],
                                  gtc=gt_ref[tok, :],
                                  gr=gtr_ref[c, pl.ds(d * LA_HEADS + hd, 1), :],
                                  lane_g=d * LA_HEADS + hd, lane_b=(2 + d) * LA_HEADS + hd,
                                  lower=(d == 0)))
                dest.append((ch * N + c, ch * T + c * CHUNK))
        for (kq, bm, o0, dec), (slot, orow) in zip(_chunk_local(probs, hooks), dest):
            kq_ref[pl.ds(pl.multiple_of(slot * KQ_ROWS, CHUNK), KQ_ROWS), :] = kq
            bm_ref[pl.ds(pl.multiple_of(slot * HEAD_DIM, HEAD_DIM), HEAD_DIM), :] = bm
            oc_ref[pl.ds(pl.multiple_of(orow, CHUNK), CHUNK), :] = o0
            dc_ref[pl.ds(pl.multiple_of(slot * 8, 8), 8), :] = jnp.broadcast_to(dec, (8, LANES))

    def state_step(m, states):
        slots = [ch * N + chain_chunk(ch, m) for ch in range(n_chains)]
        orows = [pl.ds(pl.multiple_of(ch * T + chain_chunk(ch, m) * CHUNK, CHUNK), CHUNK)
                 for ch in range(n_chains)]
        r = [jnp.dot(kq_ref[pl.ds(pl.multiple_of(slots[ch] * KQ_ROWS, CHUNK), KQ_ROWS), :],
                     states[ch].astype(BF16), preferred_element_type=F32)
             for ch in range(n_chains)]
        new = []
        for ch in range(n_chains):
            dec = dc_ref[pl.ds(pl.multiple_of(slots[ch] * 8, 8), 8), :][0:1, :]
            bm = bm_ref[pl.ds(pl.multiple_of(slots[ch] * HEAD_DIM, HEAD_DIM), HEAD_DIM), :]
            new.append(states[ch] * dec + bm - r[ch][:HEAD_DIM])
            oc_ref[orows[ch], :] = oc_ref[orows[ch], :] + r[ch][HEAD_DIM:]
        return new

    def attention_hooks(it):
        tq = qa_ref.shape[0] // n_it
        rows = pl.ds(pl.multiple_of(it * tq, tq), tq)
        n_kb = ka_ref.shape[0] // ATTN_KEY_BLOCK
        keys = lambda j: slice(j * ATTN_KEY_BLOCK, (j + 1) * ATTN_KEY_BLOCK)
        st = {}

        def scores(j):
            def run():
                if j == 0:
                    st["q"] = jnp.concatenate(
                        [qa_ref[rows, g * HEAD_DIM:(g + 1) * HEAD_DIM] for g in range(GQA_GROUP)],
                        axis=0)
                st[j] = _mm_nt(st["q"], ka_ref[keys(j), :])
            return run

        def exp_pv(j):
            def run():
                if j == 0:
                    m = jnp.max(st[0], axis=-1, keepdims=True)
                    for i in range(1, n_kb):
                        m = jnp.maximum(m, jnp.max(st[i], axis=-1, keepdims=True))
                    st["m"] = m
                p = jnp.exp2(st[j] - st["m"])
                lb = jnp.sum(p, axis=-1, keepdims=True)
                ob = _mm(p, va_ref[keys(j), :])
                st["l"], st["o"] = (lb, ob) if j == 0 else (st["l"] + lb, st["o"] + ob)
            return run

        def finish():
            o = (st["o"] / st["l"]).astype(BF16)
            for g in range(GQA_GROUP):
                oa_ref[rows, g * HEAD_DIM:(g + 1) * HEAD_DIM] = o[g * tq:(g + 1) * tq]

        return ([scores(j) for j in range(n_kb)], [exp_pv(j) for j in range(n_kb)] + [finish])

    def weave(first, second):
        out, j = [], 0
        for i, h in enumerate(first):
            out.append(h)
            while j < len(second) and (j + 1) * len(first) <= (i + 1) * len(second):
                out.append(second[j])
                j += 1
        return out + second[j:]

    sc, pv = attention_hooks(0)
    local_part(0, sc + pv)

    def body(it, states):
        st = [list(states)]

        def hook(j):
            def run():
                st[0] = state_step((it - 1) * GDN_CPI + j, st[0])
            return run

        sc, pv = attention_hooks(it)
        local_part(it, weave(sc + pv, [hook(j) for j in range(GDN_CPI)]))
        return tuple(st[0])

    zero = jnp.zeros((HEAD_DIM, HEAD_DIM), F32)
    states = list(lax.fori_loop(1, n_it, body, (zero,) * n_chains))
    for j in range(GDN_CPI):
        states = state_step((n_it - 1) * GDN_CPI + j, states)
    for hp in range(GDN_HP):
        sl = slice(hp * HEAD_DIM, (hp + 1) * HEAD_DIM)
        o = oc_ref[2 * hp * T:(2 * hp + 1) * T, :] + oc_ref[(2 * hp + 1) * T:(2 * hp + 2) * T, :]
        o_ref[:, sl] = (_rms(o, nw_ref[...]) * _silu(z_ref[:, sl])).astype(BF16)


def _mixers(qkv, z, gt, gtr, nw, qb, kb, vb):
    B, T, _ = qkv.shape
    N = T // CHUNK
    hw = GDN_HP * HEAD_DIM
    nblk = LA_HEADS // GDN_HP
    assert nblk % KV_HEADS == 0
    per_kv = nblk // KV_HEADS
    n_it = N // GDN_CPI
    ta = T // per_kv
    assert T % per_kv == 0 and ta % n_it == 0 and (ta // n_it) % 16 == 0
    assert T % ATTN_KEY_BLOCK == 0
    gw = GQA_GROUP * HEAD_DIM
    seq = lambda off: pl.BlockSpec((None, T, hw), lambda b, h: (b, 0, off + h))
    att_q = pl.BlockSpec((None, ta, gw), lambda b, h: (b, h % per_kv, h // per_kv))
    att_kv = pl.BlockSpec((None, T, HEAD_DIM), lambda b, h: (b, 0, h // per_kv))
    chains = 2 * GDN_HP
    return pl.pallas_call(
        _mixers_kernel,
        grid=(B, nblk),
        in_specs=[seq(0), seq(nblk), seq(2 * nblk),
                  seq(0),
                  pl.BlockSpec((None, T, LANES), lambda b, h: (b, 0, 0)),
                  pl.BlockSpec((None, N, 4 * LA_HEADS, LANES), lambda b, h: (b, 0, 0, 0)),
                  pl.BlockSpec((1, HEAD_DIM), lambda b, h: (0, 0)),
                  att_q, att_kv, att_kv],
        out_specs=[seq(0), att_q],
        out_shape=[jax.ShapeDtypeStruct((B, T, LA_DIM), BF16),
                   jax.ShapeDtypeStruct((B, T, ATT_Q_DIM), BF16)],
        scratch_shapes=[
            pltpu.VMEM((chains * N * KQ_ROWS, HEAD_DIM), BF16),
            pltpu.VMEM((chains * N * HEAD_DIM, HEAD_DIM), F32),
            pltpu.VMEM((chains * T, HEAD_DIM), F32),
            pltpu.VMEM((chains * N * 8, LANES), F32)],
        compiler_params=_params(2),
        name="mixers",
    )(qkv, qkv, qkv, z, gt, gtr, nw, qb, kb, vb)


def _merge_kernel(x_ref, oa_ref, ob_ref, gate_ref, wa_ref, wb_ref, wo_ref, nw_ref, o_ref):
    ya = jnp.dot(oa_ref[...], wa_ref[...], preferred_element_type=F32)
    yb = jnp.dot(ob_ref[...], wb_ref[...], preferred_element_type=F32)
    mix = gate_ref[:, :D_MODEL] * ya + gate_ref[:, D_MODEL:] * yb
    mixed = jnp.dot(mix.astype(BF16), wo_ref[...], preferred_element_type=F32)
    o_ref[...] = x_ref[...] + _rms(mixed, nw_ref[...])


def _merge(x2, oa, ob, gate, wa, wb, wo, nw, tm):
    M = x2.shape[0]
    row = lambda n: pl.BlockSpec((tm, n), lambda i: (i, 0))
    sq = _resident((D_MODEL, D_MODEL))
    return pl.pallas_call(
        _merge_kernel,
        grid=(M // tm,),
        in_specs=[row(D_MODEL), row(D_MODEL), row(D_MODEL), row(2 * D_MODEL), sq, sq, sq,
                  _resident((1, D_MODEL))],
        out_specs=row(D_MODEL),
        out_shape=jax.ShapeDtypeStruct((M, D_MODEL), F32),
        compiler_params=_params(1),
        name="merge",
    )(x2, oa, ob, gate, wa, wb, wo, nw)


def _memkv_kernel(mem_ref, nw_ref, w_ref, o_ref):
    o_ref[...] = jnp.dot(_rms(mem_ref[...], nw_ref[...]).astype(BF16), w_ref[...],
                         preferred_element_type=F32).astype(BF16)


def _memkv(mem, nw, w):
    B, Mt, _ = mem.shape
    return pl.pallas_call(
        _memkv_kernel,
        grid=(B,),
        in_specs=[pl.BlockSpec((None, Mt, D_MODEL), lambda b: (b, 0, 0)),
                  _resident((1, D_MODEL)), _resident((D_MODEL, 2 * D_MODEL))],
        out_specs=pl.BlockSpec((None, Mt, 2 * D_MODEL), lambda b: (b, 0, 0)),
        out_shape=jax.ShapeDtypeStruct((B, Mt, 2 * D_MODEL), BF16),
        compiler_params=_params(1),
        name="memkv",
    )(mem, nw, w)


def _memattn_kernel(x_ref, kv_ref, prew_ref, wq_ref, wo_ref, postw_ref, o_ref, att_ref):
    x = x_ref[...]
    h = _rms(x, prew_ref[...]).astype(BF16)
    q = (jnp.dot(h, wq_ref[...], preferred_element_type=F32)
         * (MEM_HEAD_DIM ** -0.5 * LOG2_E)).astype(BF16)
    heads = [slice(hd * MEM_HEAD_DIM, (hd + 1) * MEM_HEAD_DIM) for hd in range(MEM_HEADS)]
    scores = lambda sl: _mm_nt(q[:, sl], kv_ref[:, sl])
    s_next = scores(heads[0])
    for hd, sl in enumerate(heads):
        s = s_next
        if hd + 1 < MEM_HEADS:
            s_next = scores(heads[hd + 1])
        p = jnp.exp2(s - jnp.max(s, axis=-1, keepdims=True))
        l = jnp.sum(p, axis=-1, keepdims=True)
        v = kv_ref[:, D_MODEL + hd * MEM_HEAD_DIM:D_MODEL + (hd + 1) * MEM_HEAD_DIM]
        att_ref[:, sl] = (_mm(p, v) / l).astype(BF16)
    y = jnp.dot(att_ref[...], wo_ref[...], preferred_element_type=F32)
    o_ref[...] = x + _rms(y, postw_ref[...])


def _memattn(x2, kv, prew, wq, wo, postw, T, tm):
    M = x2.shape[0]
    Mt = kv.shape[1]
    tpb = T // tm
    row = pl.BlockSpec((tm, D_MODEL), lambda i: (i, 0))
    sq = _resident((D_MODEL, D_MODEL))
    vec = _resident((1, D_MODEL))
    return pl.pallas_call(
        _memattn_kernel,
        grid=(M // tm,),
        in_specs=[row, pl.BlockSpec((None, Mt, 2 * D_MODEL), lambda i: (i // tpb, 0, 0)),
                  vec, sq, sq, vec],
        out_specs=row,
        out_shape=jax.ShapeDtypeStruct((M, D_MODEL), F32),
        scratch_shapes=[pltpu.VMEM((tm, D_MODEL), BF16)],
        compiler_params=_params(1),
        name="memattn",
    )(x2, kv, prew, wq, wo, postw)


def _ffn_kernel(x_ref, prew_ref, wg_ref, wu_ref, wd_ref, postw_ref, o_ref, *, n_split):
    x = x_ref[...]
    h = _rms(x, prew_ref[...]).astype(BF16)
    d_ff = wg_ref.shape[1]
    cs = d_ff // n_split
    assert cs % LANES == 0 and cs * n_split == d_ff
    y = None
    for c in range(0, d_ff, cs):
        gate = jnp.dot(h, wg_ref[:, c:c + cs], preferred_element_type=F32)
        up = jnp.dot(h, wu_ref[:, c:c + cs], preferred_element_type=F32)
        part = jnp.dot((_silu(gate) * up).astype(BF16), wd_ref[c:c + cs, :],
                       preferred_element_type=F32)
        y = part if y is None else y + part
    o_ref[...] = x + _rms(y, postw_ref[...])


def _ffn(x2, prew, wg, wu, wd, postw, tm):
    M = x2.shape[0]
    d_ff = wg.shape[1]
    row = pl.BlockSpec((tm, D_MODEL), lambda i: (i, 0))
    vec = _resident((1, D_MODEL))
    return pl.pallas_call(
        functools.partial(_ffn_kernel, n_split=11),
        grid=(M // tm,),
        in_specs=[row, vec, _resident((D_MODEL, d_ff)), _resident((D_MODEL, d_ff)),
                  _resident((d_ff, D_MODEL)), vec],
        out_specs=row,
        out_shape=jax.ShapeDtypeStruct((M, D_MODEL), F32),
        compiler_params=_params(1),
        name="ffn",
    )(x2, prew, wg, wu, wd, postw)


def _rope_tables(T):
    rows = T // GRID_W
    row = jnp.repeat(jnp.arange(rows, dtype=F32), GRID_W)
    col = jnp.tile(jnp.arange(GRID_W, dtype=F32), rows)
    half = HEAD_DIM // 2
    inv_freq = ROPE_THETA ** (-jnp.arange(0, half, 2, dtype=F32) / half)
    ang_r = row[:, None] * inv_freq
    ang_c = col[:, None] * inv_freq
    cos_t = jnp.concatenate([jnp.cos(ang_r), jnp.cos(ang_r), jnp.cos(ang_c), jnp.cos(ang_c)], axis=-1)
    sin_t = jnp.concatenate([-jnp.sin(ang_r), jnp.sin(ang_r), -jnp.sin(ang_c), jnp.sin(ang_c)], axis=-1)
    return cos_t, sin_t


def _layer(x, mem, mix_pre_norm, w_in, conv_w, la_a_log, la_dt_bias, la_norm_w, w_out_a,
           q_norm_w, k_norm_w, w_out_b, b_gate, w_out, mix_post_norm,
           mem_pre_norm, mem_kv_norm, w_mq, w_mkv, w_mo, mem_post_norm,
           ffn_pre_norm, w_ffn_in, w_ffn_out, ffn_post_norm):
    B, T, D = x.shape
    assert D == D_MODEL and T % 256 == 0 and T % GRID_W == 0
    M = B * T
    N = T // CHUNK
    tm_proj = 256
    tm_tok = 512 if T % 512 == 0 else 256
    vec = lambda a: a.reshape(1, -1).astype(F32)

    a0 = 4 * LA_DIM
    b0 = a0 + 4 * LA_HEADS
    w_r = jnp.concatenate([w_in[:, :a0], w_in[:, b0:], w_in[:, a0:b0],
                           jnp.zeros((D, LANES - 4 * LA_HEADS), w_in.dtype)], axis=1).astype(BF16)
    pad16 = lambda a: jnp.concatenate([a.reshape(-1).astype(F32), jnp.zeros((LANES - 2 * LA_HEADS,), F32)]).reshape(1, LANES)
    cos_t, sin_t = _rope_tables(T)

    x2 = x.reshape(M, D)
    qkv, z, qb, kb, vb, gate, gt = _proj(x2, vec(mix_pre_norm), w_r, conv_w.astype(F32), vec(b_gate),
                                         vec(q_norm_w), vec(k_norm_w), cos_t, sin_t,
                                         pad16(la_a_log), pad16(la_dt_bias), T, tm_proj)
    gt3 = gt.reshape(B, T, LANES)
    gtr = gt3[:, :, :4 * LA_HEADS].reshape(B, N, CHUNK, 4 * LA_HEADS).transpose(0, 1, 3, 2)
    gtr = jnp.concatenate([gtr, gtr], axis=-1)
    oa, ob = _mixers(qkv.reshape(B, T, 3 * LA_DIM), z.reshape(B, T, LA_DIM), gt3, gtr,
                     vec(la_norm_w), qb.reshape(B, T, ATT_Q_DIM), kb.reshape(B, T, ATT_KV_DIM),
                     vb.reshape(B, T, ATT_KV_DIM))
    x2 = _merge(x2, oa.reshape(M, LA_DIM), ob.reshape(M, ATT_Q_DIM), gate, w_out_a.astype(BF16),
                w_out_b.astype(BF16), w_out.astype(BF16), vec(mix_post_norm), tm_tok)
    kv = _memkv(mem, vec(mem_kv_norm), w_mkv.astype(BF16))
    x2 = _memattn(x2, kv, vec(mem_pre_norm), w_mq.astype(BF16), w_mo.astype(BF16),
                  vec(mem_post_norm), T, tm_tok)
    d_ff = w_ffn_out.shape[0]
    x2 = _ffn(x2, vec(ffn_pre_norm), w_ffn_in[:, :d_ff].astype(BF16), w_ffn_in[:, d_ff:].astype(BF16),
              w_ffn_out.astype(BF16), vec(ffn_post_norm), tm_tok)
    return x2.reshape(B, T, D)


def kernel(x, mem, mix_pre_norm, w_in, conv_w, la_a_log, la_dt_bias, la_norm_w, w_out_a, q_norm_w, k_norm_w, w_out_b, b_gate, w_out, mix_post_norm, mem_pre_norm, mem_kv_norm, w_mq, w_mkv, w_mo, mem_post_norm, ffn_pre_norm, w_ffn_in, w_ffn_out, ffn_post_norm):
    depth = w_in.shape[0]
    for l in range(depth):
        x = _layer(x, mem, mix_pre_norm[l], w_in[l], conv_w[l], la_a_log[l], la_dt_bias[l],
                   la_norm_w[l], w_out_a[l], q_norm_w[l], k_norm_w[l], w_out_b[l], b_gate[l],
                   w_out[l], mix_post_norm[l], mem_pre_norm[l], mem_kv_norm[l], w_mq[l], w_mkv[l],
                   w_mo[l], mem_post_norm[l], ffn_pre_norm[l], w_ffn_in[l], w_ffn_out[l],
                   ffn_post_norm[l])
    return x
```

```python
import functools

import jax
import jax.numpy as jnp
from jax import lax
from jax.experimental import pallas as pl
from jax.experimental.pallas import tpu as pltpu

F32 = jnp.float32
BF16 = jnp.bfloat16

D_MODEL = 1024
HEAD_DIM = 128
LA_HEADS = 8
LA_DIM = LA_HEADS * HEAD_DIM
CONV_K = 5
CHUNK = 64
ATT_HEADS = 8
KV_HEADS = 2
GQA_GROUP = ATT_HEADS // KV_HEADS
ATT_Q_DIM = ATT_HEADS * HEAD_DIM
ATT_KV_DIM = KV_HEADS * HEAD_DIM
GRID_W = 64
ROPE_THETA = 10000.0
MEM_HEADS = 4
MEM_HEAD_DIM = D_MODEL // MEM_HEADS
EPS = 1e-6
LOG2_E = 1.4426950408889634

LANES = 128
VMEM_LIMIT = 56 * 1024 * 1024

C_QKV = 0
C_Z = C_QKV + 3 * LA_DIM
C_QB = C_Z + LA_DIM
C_KB = C_QB + ATT_Q_DIM
C_VB = C_KB + ATT_KV_DIM
C_GATE = C_VB + ATT_KV_DIM
C_AB = C_GATE + 2 * D_MODEL
N_PROJ = C_AB + LANES
PROJ_STEP = 256


def _mm(a, b):
    return jnp.dot(a.astype(BF16), b.astype(BF16), preferred_element_type=F32)


def _mm_nt(a, b):
    return lax.dot_general(a.astype(BF16), b.astype(BF16), (((1,), (1,)), ((), ())),
                           preferred_element_type=F32)


def _mm_tn(a, b):
    return lax.dot_general(a.astype(BF16), b.astype(BF16), (((0,), (0,)), ((), ())),
                           preferred_element_type=F32)


def _rms(x, w):
    return x * lax.rsqrt(jnp.mean(x * x, axis=-1, keepdims=True) + EPS) * w


def _sigmoid(x):
    return 1.0 / (1.0 + jnp.exp(-x))


def _silu(x):
    return x * _sigmoid(x)


def _resident(shape):
    nd = len(shape)
    return pl.BlockSpec(shape, lambda *_: (0,) * nd, pipeline_mode=pl.Buffered(1))


def _params(n_grid):
    return pltpu.CompilerParams(dimension_semantics=("arbitrary",) * n_grid,
                                vmem_limit_bytes=VMEM_LIMIT)


def _proj_kernel(x_ref, xp_ref, xn_ref, nw_ref, w_ref, cw_ref, bg_ref, qn_ref, kn_ref, cos_ref,
                 sin_ref, alog_ref, dtb_ref,
                 qkv_ref, z_ref, qb_ref, kb_ref, vb_ref, gate_ref, gt_ref, *, tiles_per_seq):
    tm = x_ref.shape[0]
    halo = xp_ref.shape[0]
    nw = nw_ref[...]
    h = _rms(x_ref[...], nw).astype(BF16)

    seq_tile = pl.program_id(0) % tiles_per_seq
    hp = jnp.where(seq_tile > 0, _rms(xp_ref[...], nw), 0.0).astype(BF16)
    hn = jnp.where(seq_tile < tiles_per_seq - 1, _rms(xn_ref[...], nw), 0.0).astype(BF16)
    he = jnp.concatenate([hp, h, hn], axis=0)
    pad = (CONV_K - 1) // 2
    step = PROJ_STEP
    cos = cos_ref[...]
    sin = sin_ref[...]
    lane = lax.broadcasted_iota(jnp.int32, (tm, LANES), 1)
    first = (lane % 64) < 32

    def norm_rope(y, nw, scale):
        y = _rms(y, nw)
        partner = jnp.where(first, pltpu.roll(y, 96, axis=1), pltpu.roll(y, 32, axis=1))
        return ((y * cos + partner * sin) * scale).astype(BF16)

    def qkv_epilogue(re, c):
        cw = cw_ref[:, c:c + step]
        acc = re[halo:halo + tm] * cw[pad:pad + 1, :]
        for j in range(CONV_K):
            s = j - pad
            if s != 0:
                shifted = pltpu.roll(re, (-s) % (tm + 2 * halo), axis=0)[halo:halo + tm]
                acc = acc + shifted * cw[j:j + 1, :]
        y = _silu(acc)
        for j in range(0, step, HEAD_DIM):
            col = c + j
            yh = y[:, j:j + HEAD_DIM]
            if col < 2 * LA_DIM:
                yh = yh * lax.rsqrt(jnp.sum(yh * yh, axis=-1, keepdims=True) + EPS)
            if col < LA_DIM:
                yh = yh * (HEAD_DIM ** -0.5)
            qkv_ref[:, col:col + HEAD_DIM] = yh.astype(BF16)

    def z_epilogue(r, c):
        z_ref[:, c:c + step] = r

    def gate_epilogue(r, c):
        gate_ref[:, c:c + step] = _sigmoid(r + bg_ref[:, c:c + step])

    def vb_epilogue(r, c):
        vb_ref[...] = r.astype(BF16)

    def qb_epilogue(r, c):
        qn = qn_ref[...]
        for j in range(0, step, HEAD_DIM):
            qb_ref[:, c + j:c + j + HEAD_DIM] = norm_rope(r[:, j:j + HEAD_DIM], qn,
                                                           HEAD_DIM ** -0.5 * LOG2_E)

    def kb_epilogue(r, c):
        kn = kn_ref[...]
        for j in range(0, ATT_KV_DIM, HEAD_DIM):
            kb_ref[:, j:j + HEAD_DIM] = norm_rope(r[:, j:j + HEAD_DIM], kn, 1.0)

    def gt_epilogue(r, c):
        xs = r + dtb_ref[...]
        softplus = jnp.maximum(xs, 0.0) + jnp.log1p(jnp.exp(-jnp.abs(xs)))
        g = -jnp.exp(alog_ref[...]) * softplus
        pos = lax.broadcasted_iota(jnp.int32, (tm, LANES), 0) % CHUNK
        pre = g
        suf = g
        s = 1
        while s < CHUNK:
            pre = pre + jnp.where(pos >= s, pltpu.roll(pre, s, axis=0), 0.0)
            suf = suf + jnp.where(pos < CHUNK - s, pltpu.roll(suf, tm - s, axis=0), 0.0)
            s *= 2
        gt_ref[...] = jnp.where(lane < LA_HEADS, pre,
                                jnp.where(lane < 2 * LA_HEADS, suf, _sigmoid(r)))

    def spread(few, many):
        out, j = [], 0
        for i, st in enumerate(many):
            while j < len(few) and j * len(many) <= i * len(few):
                out.append(few[j])
                j += 1
            out.append(st)
        assert j == len(few)
        return out

    conv = [(he, C_QKV + c, step, qkv_epilogue, c) for c in range(0, 3 * LA_DIM, step)]
    medium = [(h, C_QB + c, step, qb_epilogue, c) for c in range(0, ATT_Q_DIM, step)]
    medium += [(h, C_KB, ATT_KV_DIM, kb_epilogue, 0), (h, C_AB, LANES, gt_epilogue, 0)]
    cheap = [(h, C_GATE + c, step, gate_epilogue, c) for c in range(0, 2 * D_MODEL, step)]
    cheap += [(h, C_VB, ATT_KV_DIM, vb_epilogue, 0)]
    cheap += [(h, C_Z + c, step, z_epilogue, c) for c in range(0, LA_DIM, step)]
    stages = spread(conv, spread(medium, cheap))
    pending = None
    for lhs, c0, n, epilogue, c in stages:
        r = jnp.dot(lhs, w_ref[:, c0:c0 + n], preferred_element_type=F32)
        if pending is not None:
            pending[0](pending[1], pending[2])
        pending = (epilogue, r, c)
    pending[0](pending[1], pending[2])


def _proj(x2, nw, w, cw, bg, qn, kn, cos_t, sin_t, alog, dtb, T, tm):
    M = x2.shape[0]
    tpb = T // tm
    halo = 8
    hb = tm // halo
    row = lambda n: pl.BlockSpec((tm, n), lambda i: (i, 0))
    tab = pl.BlockSpec((tm, LANES), lambda i: (i % tpb, 0))
    prev = pl.BlockSpec((halo, D_MODEL), lambda i: (jnp.maximum(i * hb - 1, 0), 0))
    nxt = pl.BlockSpec((halo, D_MODEL), lambda i: (jnp.minimum((i + 1) * hb, M // halo - 1), 0))
    outs = [(3 * LA_DIM, BF16), (LA_DIM, F32), (ATT_Q_DIM, BF16), (ATT_KV_DIM, BF16),
            (ATT_KV_DIM, BF16), (2 * D_MODEL, F32), (LANES, F32)]
    return pl.pallas_call(
        functools.partial(_proj_kernel, tiles_per_seq=tpb),
        grid=(M // tm,),
        in_specs=[row(D_MODEL), prev, nxt, _resident((1, D_MODEL)), _resident((D_MODEL, N_PROJ)),
                  _resident((CONV_K, 3 * LA_DIM)),
                  _resident((1, 2 * D_MODEL)), _resident((1, HEAD_DIM)), _resident((1, HEAD_DIM)),
                  tab, tab, _resident((1, LANES)), _resident((1, LANES))],
        out_specs=[row(n) for n, _ in outs],
        out_shape=[jax.ShapeDtypeStruct((M, n), dt) for n, dt in outs],
        compiler_params=_params(1),
        name="proj",
    )(x2, x2, x2, nw, w, cw, bg, qn, kn, cos_t, sin_t, alog, dtb)


def _chunk_local(probs, hooks=()):
    C = CHUNK
    n = len(probs)
    assert 2 * C == LANES
    lane = lax.broadcasted_iota(jnp.int32, (C, LANES), 1)
    ii = lax.broadcasted_iota(jnp.int32, (C, LANES), 0)
    jj = lane % C
    hi = lane >= C
    eye_hi = jnp.where((ii == jj) & hi, 1.0, 0.0)
    incl = [(ii >= jj) if p["lower"] else (ii <= jj) for p in probs]
    strict = [(ii > jj) if p["lower"] else (ii < jj) for p in probs]
    col = lambda p, l: jnp.sum(jnp.where(lane == l, p["gtc"], 0.0), axis=1, keepdims=True)
    gc = [col(p, p["lane_g"]) for p in probs]
    beta = [col(p, p["lane_b"]) for p in probs]
    decay = [jnp.where(incl[i], jnp.exp(jnp.where(incl[i], gc[i] - probs[i]["gr"], 0.0)), 0.0)
             for i in range(n)]
    hooks = list(hooks)
    n_stages = 10
    done = [0, 0]

    def stage_done():
        done[0] += 1
        while done[1] < len(hooks) and (done[1] + 1) * n_stages <= done[0] * (len(hooks) + 1):
            hooks[done[1]]()
            done[1] += 1

    gram, f32 = {}, {}
    for p in probs:
        if p["cid"] not in gram:
            kk = jnp.concatenate([p["k"], p["k"]], axis=0)
            gram[p["cid"]] = lax.dot_general(
                jnp.concatenate([p["k"], p["q"]], axis=0), kk,
                (((1,), (1,)), ((), ())), preferred_element_type=F32)
            f32[p["cid"]] = tuple(p[name].astype(F32) for name in ("q", "k", "v"))
    stage_done()
    qf = [f32[p["cid"]][0] for p in probs]
    kf = [f32[p["cid"]][1] for p in probs]
    vf = [f32[p["cid"]][2] for p in probs]
    aq = [gram[p["cid"]] for p in probs]
    kb = [kf[i] * beta[i] for i in range(n)]
    neg_l = [jnp.where(strict[i], -(aq[i][:C] * beta[i] * decay[i]), 0.0) for i in range(n)]
    qk = [(aq[i][C:] * decay[i])[:, :C].astype(BF16) for i in range(n)]
    nb = [x.astype(BF16) for x in neg_l]
    sq = [jnp.dot(x[:, :C], x, preferred_element_type=F32) for x in nb]
    stage_done()
    z = [jnp.where(hi, eye_hi + neg_l[i], sq[i]) for i in range(n)]
    n_terms = 2
    while n_terms < C // 2:
        zb = [x.astype(BF16) for x in z]
        r = [jnp.dot(x[:, :C], x, preferred_element_type=F32) for x in zb]
        stage_done()
        z = [r[i] + jnp.where(hi, z[i], 0.0) for i in range(n)]
        n_terms *= 2
    zb = [x.astype(BF16) for x in z]
    pw = [x[:, :C] for x in zb]
    eg = [jnp.exp(x) for x in gc]
    rhs = [jnp.concatenate([vf[i] * beta[i], kb[i] * eg[i]], axis=1).astype(BF16)
           for i in range(n)]
    y = [jnp.dot(zb[i][:, C:], rhs[i], preferred_element_type=F32) for i in range(n)]
    stage_done()
    y = [(y[i] + jnp.dot(pw[i], y[i].astype(BF16), preferred_element_type=F32)).astype(BF16)
         for i in range(n)]
    stage_done()
    gl = [p["gr"][:, C - 1:C] if p["lower"] else p["gr"][:, 0:1] for p in probs]
    kd = [(kf[i] * jnp.exp(gl[i] - gc[i])).astype(BF16) for i in range(n)]
    r1 = [lax.dot_general(kd[i], y[i], (((0,), (0,)), ((), ())), preferred_element_type=F32)
          for i in range(n)]
    stage_done()
    r2 = [jnp.dot(qk[i], y[i], preferred_element_type=F32) for i in range(n)]
    stage_done()
    assert done == [n_stages, len(hooks)]
    out = []
    for i in range(n):
        qp = qf[i] * eg[i] - r2[i][:, HEAD_DIM:]
        kq = jnp.concatenate([r1[i][:, HEAD_DIM:], qp], axis=0).astype(BF16)
        out.append((kq, r1[i][:, :HEAD_DIM], r2[i][:, :HEAD_DIM], jnp.exp(gl[i])))
    return out


GDN_HP = 2
GDN_CPI = 4
KQ_ROWS = HEAD_DIM + CHUNK


ATTN_KEY_BLOCK = 512


def _mixers_kernel(q_ref, k_ref, v_ref, z_ref, gt_ref, gtr_ref, nw_ref, qa_ref, ka_ref, va_ref,
                   o_ref, oa_ref, kq_ref, bm_ref, oc_ref, dc_ref):
    T = q_ref.shape[0]
    N = T // CHUNK
    n_it = N // GDN_CPI
    n_chains = 2 * GDN_HP
    hd0 = pl.program_id(1) * GDN_HP

    def chain_chunk(ch, m):
        return m if ch % 2 == 0 else N - 1 - m

    def local_part(it, hooks):
        probs, dest = [], []
        for ch in range(n_chains):
            hp, d = divmod(ch, 2)
            hd = hd0 + hp
            sl = slice(hp * HEAD_DIM, (hp + 1) * HEAD_DIM)
            for j in range(GDN_CPI):
                c = chain_chunk(ch, it * GDN_CPI + j)
                tok = pl.ds(pl.multiple_of(c * CHUNK, CHUNK), CHUNK)
                probs.append(dict(cid=(ch, j), q=q_ref[tok, sl], k=k_ref[tok, sl], v=v_ref[tok, sl],
                                  gtc=gt_ref[tok, :],
                                  gr=gtr_ref[c, pl.ds(d * LA_HEADS + hd, 1), :],
                                  lane_g=d * LA_HEADS + hd, lane_b=(2 + d) * LA_HEADS + hd,
                                  lower=(d == 0)))
                dest.append((ch * N + c, ch * T + c * CHUNK))
        for (kq, bm, o0, dec), (slot, orow) in zip(_chunk_local(probs, hooks), dest):
            kq_ref[pl.ds(pl.multiple_of(slot * KQ_ROWS, CHUNK), KQ_ROWS), :] = kq
            bm_ref[pl.ds(pl.multiple_of(slot * HEAD_DIM, HEAD_DIM), HEAD_DIM), :] = bm
            oc_ref[pl.ds(pl.multiple_of(orow, CHUNK), CHUNK), :] = o0
            dc_ref[pl.ds(pl.multiple_of(slot * 8, 8), 8), :] = jnp.broadcast_to(dec, (8, LANES))

    def state_step(m, states):
        slots = [ch * N + chain_chunk(ch, m) for ch in range(n_chains)]
        orows = [pl.ds(pl.multiple_of(ch * T + chain_chunk(ch, m) * CHUNK, CHUNK), CHUNK)
                 for ch in range(n_chains)]
        r = [jnp.dot(kq_ref[pl.ds(pl.multiple_of(slots[ch] * KQ_ROWS, CHUNK), KQ_ROWS), :],
                     states[ch].astype(BF16), preferred_element_type=F32)
             for ch in range(n_chains)]
        new = []
        for ch in range(n_chains):
            dec = dc_ref[pl.ds(pl.multiple_of(slots[ch] * 8, 8), 8), :][0:1, :]
            bm = bm_ref[pl.ds(pl.multiple_of(slots[ch] * HEAD_DIM, HEAD_DIM), HEAD_DIM), :]
            new.append(states[ch] * dec + bm - r[ch][:HEAD_DIM])
            oc_ref[orows[ch], :] = oc_ref[orows[ch], :] + r[ch][HEAD_DIM:]
        return new

    def attention_hooks(it):
        tq = qa_ref.shape[0] // n_it
        rows = pl.ds(pl.multiple_of(it * tq, tq), tq)
        n_kb = ka_ref.shape[0] // ATTN_KEY_BLOCK
        keys = lambda j: slice(j * ATTN_KEY_BLOCK, (j + 1) * ATTN_KEY_BLOCK)
        st = {}

        def scores(j):
            def run():
                if j == 0:
                    st["q"] = jnp.concatenate(
                        [qa_ref[rows, g * HEAD_DIM:(g + 1) * HEAD_DIM] for g in range(GQA_GROUP)],
                        axis=0)
                st[j] = _mm_nt(st["q"], ka_ref[keys(j), :])
            return run

        def exp_pv(j):
            def run():
                if j == 0:
                    m = jnp.max(st[0], axis=-1, keepdims=True)
                    for i in range(1, n_kb):
                        m = jnp.maximum(m, jnp.max(st[i], axis=-1, keepdims=True))
                    st["m"] = m
                p = jnp.exp2(st[j] - st["m"])
                lb = jnp.sum(p, axis=-1, keepdims=True)
                ob = _mm(p, va_ref[keys(j), :])
                st["l"], st["o"] = (lb, ob) if j == 0 else (st["l"] + lb, st["o"] + ob)
            return run

        def finish():
            o = (st["o"] / st["l"]).astype(BF16)
            for g in range(GQA_GROUP):
                oa_ref[rows, g * HEAD_DIM:(g + 1) * HEAD_DIM] = o[g * tq:(g + 1) * tq]

        return ([scores(j) for j in range(n_kb)], [exp_pv(j) for j in range(n_kb)] + [finish])

    def weave(first, second):
        out, j = [], 0
        for i, h in enumerate(first):
            out.append(h)
            while j < len(second) and (j + 1) * len(first) <= (i + 1) * len(second):
                out.append(second[j])
                j += 1
        return out + second[j:]

    sc, pv = attention_hooks(0)
    local_part(0, sc + pv)

    def body(it, states):
        st = [list(states)]

        def hook(j):
            def run():
                st[0] = state_step((it - 1) * GDN_CPI + j, st[0])
            return run

        sc, pv = attention_hooks(it)
        local_part(it, weave(sc + pv, [hook(j) for j in range(GDN_CPI)]))
        return tuple(st[0])

    zero = jnp.zeros((HEAD_DIM, HEAD_DIM), F32)
    states = list(lax.fori_loop(1, n_it, body, (zero,) * n_chains))
    for j in range(GDN_CPI):
        states = state_step((n_it - 1) * GDN_CPI + j, states)
    for hp in range(GDN_HP):
        sl = slice(hp * HEAD_DIM, (hp + 1) * HEAD_DIM)
        o = oc_ref[2 * hp * T:(2 * hp + 1) * T, :] + oc_ref[(2 * hp + 1) * T:(2 * hp + 2) * T, :]
        o_ref[:, sl] = (_rms(o, nw_ref[...]) * _silu(z_ref[:, sl])).astype(BF16)


def _mixers(qkv, z, gt, gtr, nw, qb, kb, vb):
    B, T, _ = qkv.shape
    N = T // CHUNK
    hw = GDN_HP * HEAD_DIM
    nblk = LA_HEADS // GDN_HP
    assert nblk % KV_HEADS == 0
    per_kv = nblk // KV_HEADS
    n_it = N // GDN_CPI
    ta = T // per_kv
    assert T % per_kv == 0 and ta % n_it == 0 and (ta // n_it) % 16 == 0
    assert T % ATTN_KEY_BLOCK == 0
    gw = GQA_GROUP * HEAD_DIM
    seq = lambda off: pl.BlockSpec((None, T, hw), lambda b, h: (b, 0, off + h))
    att_q = pl.BlockSpec((None, ta, gw), lambda b, h: (b, h % per_kv, h // per_kv))
    att_kv = pl.BlockSpec((None, T, HEAD_DIM), lambda b, h: (b, 0, h // per_kv))
    chains = 2 * GDN_HP
    return pl.pallas_call(
        _mixers_kernel,
        grid=(B, nblk),
        in_specs=[seq(0), seq(nblk), seq(2 * nblk),
                  seq(0),
                  pl.BlockSpec((None, T, LANES), lambda b, h: (b, 0, 0)),
                  pl.BlockSpec((None, N, 4 * LA_HEADS, LANES), lambda b, h: (b, 0, 0, 0)),
                  pl.BlockSpec((1, HEAD_DIM), lambda b, h: (0, 0)),
                  att_q, att_kv, att_kv],
        out_specs=[seq(0), att_q],
        out_shape=[jax.ShapeDtypeStruct((B, T, LA_DIM), BF16),
                   jax.ShapeDtypeStruct((B, T, ATT_Q_DIM), BF16)],
        scratch_shapes=[
            pltpu.VMEM((chains * N * KQ_ROWS, HEAD_DIM), BF16),
            pltpu.VMEM((chains * N * HEAD_DIM, HEAD_DIM), F32),
            pltpu.VMEM((chains * T, HEAD_DIM), F32),
            pltpu.VMEM((chains * N * 8, LANES), F32)],
        compiler_params=_params(2),
        name="mixers",
    )(qkv, qkv, qkv, z, gt, gtr, nw, qb, kb, vb)


def _merge_memattn_kernel(x_ref, oa_ref, ob_ref, gate_ref, wa_ref, wb_ref, wo_ref, nw_ref,
                          kv_ref, prew_ref, wq_ref, wmo_ref, postw_ref, o_ref, att_ref):
    ya = jnp.dot(oa_ref[...], wa_ref[...], preferred_element_type=F32)
    yb = jnp.dot(ob_ref[...], wb_ref[...], preferred_element_type=F32)
    mix = gate_ref[:, :D_MODEL] * ya + gate_ref[:, D_MODEL:] * yb
    mixed = jnp.dot(mix.astype(BF16), wo_ref[...], preferred_element_type=F32)
    x = x_ref[...] + _rms(mixed, nw_ref[...])

    h = _rms(x, prew_ref[...]).astype(BF16)
    q = (jnp.dot(h, wq_ref[...], preferred_element_type=F32)
         * (MEM_HEAD_DIM ** -0.5 * LOG2_E)).astype(BF16)
    heads = [slice(hd * MEM_HEAD_DIM, (hd + 1) * MEM_HEAD_DIM) for hd in range(MEM_HEADS)]
    scores = lambda sl: _mm_nt(q[:, sl], kv_ref[:, sl])
    s_next = scores(heads[0])
    for hd, sl in enumerate(heads):
        s = s_next
        if hd + 1 < MEM_HEADS:
            s_next = scores(heads[hd + 1])
        p = jnp.exp2(s - jnp.max(s, axis=-1, keepdims=True))
        l = jnp.sum(p, axis=-1, keepdims=True)
        v = kv_ref[:, D_MODEL + hd * MEM_HEAD_DIM:D_MODEL + (hd + 1) * MEM_HEAD_DIM]
        att_ref[:, sl] = (_mm(p, v) / l).astype(BF16)
    y = jnp.dot(att_ref[...], wmo_ref[...], preferred_element_type=F32)
    o_ref[...] = x + _rms(y, postw_ref[...])


def _merge_memattn(x2, oa, ob, gate, wa, wb, wo, nw, kv, prew, wq, wmo, postw, T, tm):
    M = x2.shape[0]
    Mt = kv.shape[1]
    tpb = T // tm
    row = lambda n: pl.BlockSpec((tm, n), lambda i: (i, 0))
    sq = _resident((D_MODEL, D_MODEL))
    vec = _resident((1, D_MODEL))
    return pl.pallas_call(
        _merge_memattn_kernel,
        grid=(M // tm,),
        in_specs=[row(D_MODEL), row(D_MODEL), row(D_MODEL), row(2 * D_MODEL), sq, sq, sq, vec,
                  pl.BlockSpec((None, Mt, 2 * D_MODEL), lambda i: (i // tpb, 0, 0)),
                  vec, sq, sq, vec],
        out_specs=row(D_MODEL),
        out_shape=jax.ShapeDtypeStruct((M, D_MODEL), F32),
        scratch_shapes=[pltpu.VMEM((tm, D_MODEL), BF16)],
        compiler_params=_params(1),
        name="merge_memattn",
    )(x2, oa, ob, gate, wa, wb, wo, nw, kv, prew, wq, wmo, postw)


def _memkv_kernel(mem_ref, nw_ref, w_ref, o_ref):
    o_ref[...] = jnp.dot(_rms(mem_ref[...], nw_ref[...]).astype(BF16), w_ref[...],
                         preferred_element_type=F32).astype(BF16)


def _memkv(mem, nw, w):
    B, Mt, _ = mem.shape
    return pl.pallas_call(
        _memkv_kernel,
        grid=(B,),
        in_specs=[pl.BlockSpec((None, Mt, D_MODEL), lambda b: (b, 0, 0)),
                  _resident((1, D_MODEL)), _resident((D_MODEL, 2 * D_MODEL))],
        out_specs=pl.BlockSpec((None, Mt, 2 * D_MODEL), lambda b: (b, 0, 0)),
        out_shape=jax.ShapeDtypeStruct((B, Mt, 2 * D_MODEL), BF16),
        compiler_params=_params(1),
        name="memkv",
    )(mem, nw, w)


def _ffn_kernel(x_ref, prew_ref, wg_ref, wu_ref, wd_ref, postw_ref, o_ref, *, n_split):
    x = x_ref[...]
    h = _rms(x, prew_ref[...]).astype(BF16)
    d_ff = wg_ref.shape[1]
    cs = d_ff // n_split
    assert cs % LANES == 0 and cs * n_split == d_ff
    y = None
    for c in range(0, d_ff, cs):
        gate = jnp.dot(h, wg_ref[:, c:c + cs], preferred_element_type=F32)
        up = jnp.dot(h, wu_ref[:, c:c + cs], preferred_element_type=F32)
        part = jnp.dot((_silu(gate) * up).astype(BF16), wd_ref[c:c + cs, :],
                       preferred_element_type=F32)
        y = part if y is None else y + part
    o_ref[...] = x + _rms(y, postw_ref[...])


def _ffn(x2, prew, wg, wu, wd, postw, tm):
    M = x2.shape[0]
    d_ff = wg.shape[1]
    row = pl.BlockSpec((tm, D_MODEL), lambda i: (i, 0))
    vec = _resident((1, D_MODEL))
    return pl.pallas_call(
        functools.partial(_ffn_kernel, n_split=11),
        grid=(M // tm,),
        in_specs=[row, vec, _resident((D_MODEL, d_ff)), _resident((D_MODEL, d_ff)),
                  _resident((d_ff, D_MODEL)), vec],
        out_specs=row,
        out_shape=jax.ShapeDtypeStruct((M, D_MODEL), F32),
        compiler_params=_params(1),
        name="ffn",
    )(x2, prew, wg, wu, wd, postw)


def _rope_tables(T):
    rows = T // GRID_W
    row = jnp.repeat(jnp.arange(rows, dtype=F32), GRID_W)
    col = jnp.tile(jnp.arange(GRID_W, dtype=F32), rows)
    half = HEAD_DIM // 2
    inv_freq = ROPE_THETA ** (-jnp.arange(0, half, 2, dtype=F32) / half)
    ang_r = row[:, None] * inv_freq
    ang_c = col[:, None] * inv_freq
    cos_t = jnp.concatenate([jnp.cos(ang_r), jnp.cos(ang_r), jnp.cos(ang_c), jnp.cos(ang_c)], axis=-1)
    sin_t = jnp.concatenate([-jnp.sin(ang_r), jnp.sin(ang_r), -jnp.sin(ang_c), jnp.sin(ang_c)], axis=-1)
    return cos_t, sin_t


def _layer(x, mem, mix_pre_norm, w_in, conv_w, la_a_log, la_dt_bias, la_norm_w, w_out_a,
           q_norm_w, k_norm_w, w_out_b, b_gate, w_out, mix_post_norm,
           mem_pre_norm, mem_kv_norm, w_mq, w_mkv, w_mo, mem_post_norm,
           ffn_pre_norm, w_ffn_in, w_ffn_out, ffn_post_norm):
    B, T, D = x.shape
    assert D == D_MODEL and T % 256 == 0 and T % GRID_W == 0
    M = B * T
    N = T // CHUNK
    tm_proj = 256
    tm_tok = 512 if T % 512 == 0 else 256
    vec = lambda a: a.reshape(1, -1).astype(F32)

    a0 = 4 * LA_DIM
    b0 = a0 + 4 * LA_HEADS
    w_r = jnp.concatenate([w_in[:, :a0], w_in[:, b0:], w_in[:, a0:b0],
                           jnp.zeros((D, LANES - 4 * LA_HEADS), w_in.dtype)], axis=1).astype(BF16)
    pad16 = lambda a: jnp.concatenate([a.reshape(-1).astype(F32), jnp.zeros((LANES - 2 * LA_HEADS,), F32)]).reshape(1, LANES)
    cos_t, sin_t = _rope_tables(T)

    x2 = x.reshape(M, D)
    qkv, z, qb, kb, vb, gate, gt = _proj(x2, vec(mix_pre_norm), w_r, conv_w.astype(F32), vec(b_gate),
                                         vec(q_norm_w), vec(k_norm_w), cos_t, sin_t,
                                         pad16(la_a_log), pad16(la_dt_bias), T, tm_proj)
    gt3 = gt.reshape(B, T, LANES)
    gtr = gt3[:, :, :4 * LA_HEADS].reshape(B, N, CHUNK, 4 * LA_HEADS).transpose(0, 1, 3, 2)
    gtr = jnp.concatenate([gtr, gtr], axis=-1)
    oa, ob = _mixers(qkv.reshape(B, T, 3 * LA_DIM), z.reshape(B, T, LA_DIM), gt3, gtr,
                     vec(la_norm_w), qb.reshape(B, T, ATT_Q_DIM), kb.reshape(B, T, ATT_KV_DIM),
                     vb.reshape(B, T, ATT_KV_DIM))
    kv = _memkv(mem, vec(mem_kv_norm), w_mkv.astype(BF16))
    x2 = _merge_memattn(x2, oa.reshape(M, LA_DIM), ob.reshape(M, ATT_Q_DIM), gate,
                        w_out_a.astype(BF16), w_out_b.astype(BF16), w_out.astype(BF16),
                        vec(mix_post_norm), kv, vec(mem_pre_norm), w_mq.astype(BF16),
                        w_mo.astype(BF16), vec(mem_post_norm), T, tm_tok)
    d_ff = w_ffn_out.shape[0]
    x2 = _ffn(x2, vec(ffn_pre_norm), w_ffn_in[:, :d_ff].astype(BF16), w_ffn_in[:, d_ff:].astype(BF16),
              w_ffn_out.astype(BF16), vec(ffn_post_norm), tm_tok)
    return x2.reshape(B, T, D)


def kernel(x, mem, mix_pre_norm, w_in, conv_w, la_a_log, la_dt_bias, la_norm_w, w_out_a, q_norm_w, k_norm_w, w_out_b, b_gate, w_out, mix_post_norm, mem_pre_norm, mem_kv_norm, w_mq, w_mkv, w_mo, mem_post_norm, ffn_pre_norm, w_ffn_in, w_ffn_out, ffn_post_norm):
    depth = w_in.shape[0]
    for l in range(depth):
        x = _layer(x, mem, mix_pre_norm[l], w_in[l], conv_w[l], la_a_log[l], la_dt_bias[l],
                   la_norm_w[l], w_out_a[l], q_norm_w[l], k_norm_w[l], w_out_b[l], b_gate[l],
                   w_out[l], mix_post_norm[l], mem_pre_norm[l], mem_kv_norm[l], w_mq[l], w_mkv[l],
                   w_mo[l], mem_post_norm[l], ffn_pre_norm[l], w_ffn_in[l], w_ffn_out[l],
                   ffn_post_norm[l])
    return x
```

```python
import functools

import jax
import jax.numpy as jnp
from jax import lax
from jax.experimental import pallas as pl
from jax.experimental.pallas import tpu as pltpu

F32 = jnp.float32
BF16 = jnp.bfloat16

D_MODEL = 1024
HEAD_DIM = 128
LA_HEADS = 8
LA_DIM = LA_HEADS * HEAD_DIM
CONV_K = 5
CHUNK = 64
ATT_HEADS = 8
KV_HEADS = 2
GQA_GROUP = ATT_HEADS // KV_HEADS
ATT_Q_DIM = ATT_HEADS * HEAD_DIM
ATT_KV_DIM = KV_HEADS * HEAD_DIM
GRID_W = 64
ROPE_THETA = 10000.0
MEM_HEADS = 4
MEM_HEAD_DIM = D_MODEL // MEM_HEADS
EPS = 1e-6
LOG2_E = 1.4426950408889634

LANES = 128
VMEM_LIMIT = 56 * 1024 * 1024

C_QKV = 0
C_Z = C_QKV + 3 * LA_DIM
C_QB = C_Z + LA_DIM
C_KB = C_QB + ATT_Q_DIM
C_VB = C_KB + ATT_KV_DIM
C_GATE = C_VB + ATT_KV_DIM
C_AB = C_GATE + 2 * D_MODEL
N_PROJ = C_AB + LANES
PROJ_STEP = 256


def _mm(a, b):
    return jnp.dot(a.astype(BF16), b.astype(BF16), preferred_element_type=F32)


def _mm_nt(a, b):
    return lax.dot_general(a.astype(BF16), b.astype(BF16), (((1,), (1,)), ((), ())),
                           preferred_element_type=F32)


def _mm_tn(a, b):
    return lax.dot_general(a.astype(BF16), b.astype(BF16), (((0,), (0,)), ((), ())),
                           preferred_element_type=F32)


def _rms(x, w):
    return x * lax.rsqrt(jnp.mean(x * x, axis=-1, keepdims=True) + EPS) * w


def _sigmoid(x):
    return jax.nn.sigmoid(x)


def _silu(x):
    return x * _sigmoid(x)


def _resident(shape):
    nd = len(shape)
    return pl.BlockSpec(shape, lambda *_: (0,) * nd, pipeline_mode=pl.Buffered(1))


def _params(n_grid):
    return pltpu.CompilerParams(dimension_semantics=("arbitrary",) * n_grid,
                                vmem_limit_bytes=VMEM_LIMIT)


def _proj_kernel(x_ref, xp_ref, xn_ref, nw_ref, w_ref, cw_ref, bg_ref, qn_ref, kn_ref, cos_ref,
                 sin_ref, alog_ref, dtb_ref,
                 qkv_ref, z_ref, qb_ref, kb_ref, vb_ref, gate_ref, gt_ref, *, tiles_per_seq):
    tm = x_ref.shape[0]
    halo = xp_ref.shape[0]
    nw = nw_ref[...]
    h = _rms(x_ref[...], nw).astype(BF16)

    seq_tile = pl.program_id(0) % tiles_per_seq
    hp = jnp.where(seq_tile > 0, _rms(xp_ref[...], nw), 0.0).astype(BF16)
    hn = jnp.where(seq_tile < tiles_per_seq - 1, _rms(xn_ref[...], nw), 0.0).astype(BF16)
    he = jnp.concatenate([hp, h, hn], axis=0)
    pad = (CONV_K - 1) // 2
    step = PROJ_STEP
    cos = cos_ref[...]
    sin = sin_ref[...]
    lane = lax.broadcasted_iota(jnp.int32, (tm, LANES), 1)
    first = (lane % 64) < 32

    def norm_rope(y, nw, scale):
        y = _rms(y, nw)
        partner = jnp.where(first, pltpu.roll(y, 96, axis=1), pltpu.roll(y, 32, axis=1))
        return ((y * cos + partner * sin) * scale).astype(BF16)

    def qkv_epilogue(re, c):
        cw = cw_ref[:, c:c + step]
        acc = re[halo:halo + tm] * cw[pad:pad + 1, :]
        for j in range(CONV_K):
            s = j - pad
            if s != 0:
                shifted = pltpu.roll(re, (-s) % (tm + 2 * halo), axis=0)[halo:halo + tm]
                acc = acc + shifted * cw[j:j + 1, :]
        y = _silu(acc)
        for j in range(0, step, HEAD_DIM):
            col = c + j
            yh = y[:, j:j + HEAD_DIM]
            if col < 2 * LA_DIM:
                yh = yh * lax.rsqrt(jnp.sum(yh * yh, axis=-1, keepdims=True) + EPS)
            if col < LA_DIM:
                yh = yh * (HEAD_DIM ** -0.5)
            qkv_ref[:, col:col + HEAD_DIM] = yh.astype(BF16)

    def z_epilogue(r, c):
        z_ref[:, c:c + step] = r

    def gate_epilogue(r, c):
        gate_ref[:, c:c + step] = r + bg_ref[:, c:c + step]

    def vb_epilogue(r, c):
        vb_ref[...] = r.astype(BF16)

    def qb_epilogue(r, c):
        qn = qn_ref[...]
        for j in range(0, step, HEAD_DIM):
            qb_ref[:, c + j:c + j + HEAD_DIM] = norm_rope(r[:, j:j + HEAD_DIM], qn,
                                                           HEAD_DIM ** -0.5 * LOG2_E)

    def kb_epilogue(r, c):
        kn = kn_ref[...]
        for j in range(0, ATT_KV_DIM, HEAD_DIM):
            kb_ref[:, j:j + HEAD_DIM] = norm_rope(r[:, j:j + HEAD_DIM], kn, 1.0)

    def gt_epilogue(r, c):
        xs = r + dtb_ref[...]
        softplus = jnp.maximum(xs, 0.0) + jnp.log1p(jnp.exp(-jnp.abs(xs)))
        g = -jnp.exp(alog_ref[...]) * softplus
        pos = lax.broadcasted_iota(jnp.int32, (tm, LANES), 0) % CHUNK
        pre = g
        suf = g
        s = 1
        while s < CHUNK:
            pre = pre + jnp.where(pos >= s, pltpu.roll(pre, s, axis=0), 0.0)
            suf = suf + jnp.where(pos < CHUNK - s, pltpu.roll(suf, tm - s, axis=0), 0.0)
            s *= 2
        gt_ref[...] = jnp.where(lane < LA_HEADS, pre,
                                jnp.where(lane < 2 * LA_HEADS, suf, _sigmoid(r)))

    def spread(few, many):
        out, j = [], 0
        for i, st in enumerate(many):
            while j < len(few) and j * len(many) <= i * len(few):
                out.append(few[j])
                j += 1
            out.append(st)
        assert j == len(few)
        return out

    conv = [(he, C_QKV + c, step, qkv_epilogue, c) for c in range(0, 3 * LA_DIM, step)]
    medium = [(h, C_QB + c, step, qb_epilogue, c) for c in range(0, ATT_Q_DIM, step)]
    medium += [(h, C_KB, ATT_KV_DIM, kb_epilogue, 0), (h, C_AB, LANES, gt_epilogue, 0)]
    cheap = [(h, C_GATE + c, step, gate_epilogue, c) for c in range(0, 2 * D_MODEL, step)]
    cheap += [(h, C_VB, ATT_KV_DIM, vb_epilogue, 0)]
    cheap += [(h, C_Z + c, step, z_epilogue, c) for c in range(0, LA_DIM, step)]
    stages = spread(conv, spread(medium, cheap))
    pending = None
    for lhs, c0, n, epilogue, c in stages:
        r = jnp.dot(lhs, w_ref[:, c0:c0 + n], preferred_element_type=F32)
        if pending is not None:
            pending[0](pending[1], pending[2])
        pending = (epilogue, r, c)
    pending[0](pending[1], pending[2])


def _proj(x2, nw, w, cw, bg, qn, kn, cos_t, sin_t, alog, dtb, T, tm):
    M = x2.shape[0]
    tpb = T // tm
    halo = 8
    hb = tm // halo
    row = lambda n: pl.BlockSpec((tm, n), lambda i: (i, 0))
    tab = pl.BlockSpec((tm, LANES), lambda i: (i % tpb, 0))
    prev = pl.BlockSpec((halo, D_MODEL), lambda i: (jnp.maximum(i * hb - 1, 0), 0))
    nxt = pl.BlockSpec((halo, D_MODEL), lambda i: (jnp.minimum((i + 1) * hb, M // halo - 1), 0))
    outs = [(3 * LA_DIM, BF16), (LA_DIM, F32), (ATT_Q_DIM, BF16), (ATT_KV_DIM, BF16),
            (ATT_KV_DIM, BF16), (2 * D_MODEL, F32), (LANES, F32)]
    return pl.pallas_call(
        functools.partial(_proj_kernel, tiles_per_seq=tpb),
        grid=(M // tm,),
        in_specs=[row(D_MODEL), prev, nxt, _resident((1, D_MODEL)), _resident((D_MODEL, N_PROJ)),
                  _resident((CONV_K, 3 * LA_DIM)),
                  _resident((1, 2 * D_MODEL)), _resident((1, HEAD_DIM)), _resident((1, HEAD_DIM)),
                  tab, tab, _resident((1, LANES)), _resident((1, LANES))],
        out_specs=[row(n) for n, _ in outs],
        out_shape=[jax.ShapeDtypeStruct((M, n), dt) for n, dt in outs],
        compiler_params=_params(1),
        name="proj",
    )(x2, x2, x2, nw, w, cw, bg, qn, kn, cos_t, sin_t, alog, dtb)


def _chunk_local(probs, hooks=()):
    C = CHUNK
    n = len(probs)
    assert 2 * C == LANES
    lane = lax.broadcasted_iota(jnp.int32, (C, LANES), 1)
    ii = lax.broadcasted_iota(jnp.int32, (C, LANES), 0)
    jj = lane % C
    hi = lane >= C
    eye_hi = jnp.where((ii == jj) & hi, 1.0, 0.0)
    incl = [(ii >= jj) if p["lower"] else (ii <= jj) for p in probs]
    strict = [(ii > jj) if p["lower"] else (ii < jj) for p in probs]
    col = lambda p, l: jnp.sum(jnp.where(lane == l, p["gtc"], 0.0), axis=1, keepdims=True)
    gc = [col(p, p["lane_g"]) for p in probs]
    beta = [col(p, p["lane_b"]) for p in probs]
    decay = [jnp.where(incl[i], jnp.exp(jnp.where(incl[i], gc[i] - probs[i]["gr"], 0.0)), 0.0)
             for i in range(n)]
    hooks = list(hooks)
    n_stages = 10
    done = [0, 0]

    def stage_done():
        done[0] += 1
        while done[1] < len(hooks) and (done[1] + 1) * n_stages <= done[0] * (len(hooks) + 1):
            hooks[done[1]]()
            done[1] += 1

    gram, f32 = {}, {}
    for p in probs:
        if p["cid"] not in gram:
            kk = jnp.concatenate([p["k"], p["k"]], axis=0)
            gram[p["cid"]] = lax.dot_general(
                jnp.concatenate([p["k"], p["q"]], axis=0), kk,
                (((1,), (1,)), ((), ())), preferred_element_type=F32)
            f32[p["cid"]] = tuple(p[name].astype(F32) for name in ("q", "k", "v"))
    stage_done()
    qf = [f32[p["cid"]][0] for p in probs]
    kf = [f32[p["cid"]][1] for p in probs]
    vf = [f32[p["cid"]][2] for p in probs]
    aq = [gram[p["cid"]] for p in probs]
    kb = [kf[i] * beta[i] for i in range(n)]
    neg_l = [jnp.where(strict[i], -(aq[i][:C] * beta[i] * decay[i]), 0.0) for i in range(n)]
    qk = [(aq[i][C:] * decay[i])[:, :C].astype(BF16) for i in range(n)]
    nb = [x.astype(BF16) for x in neg_l]
    sq = [jnp.dot(x[:, :C], x, preferred_element_type=F32) for x in nb]
    stage_done()
    z = [jnp.where(hi, eye_hi + neg_l[i], sq[i]) for i in range(n)]
    n_terms = 2
    while n_terms < C // 2:
        zb = [x.astype(BF16) for x in z]
        r = [jnp.dot(x[:, :C], x, preferred_element_type=F32) for x in zb]
        stage_done()
        z = [r[i] + jnp.where(hi, z[i], 0.0) for i in range(n)]
        n_terms *= 2
    zb = [x.astype(BF16) for x in z]
    pw = [x[:, :C] for x in zb]
    eg = [jnp.exp(x) for x in gc]
    rhs = [jnp.concatenate([vf[i] * beta[i], kb[i] * eg[i]], axis=1).astype(BF16)
           for i in range(n)]
    y = [jnp.dot(zb[i][:, C:], rhs[i], preferred_element_type=F32) for i in range(n)]
    stage_done()
    y = [(y[i] + jnp.dot(pw[i], y[i].astype(BF16), preferred_element_type=F32)).astype(BF16)
         for i in range(n)]
    stage_done()
    gl = [p["gr"][:, C - 1:C] if p["lower"] else p["gr"][:, 0:1] for p in probs]
    kd = [(kf[i] * jnp.exp(gl[i] - gc[i])).astype(BF16) for i in range(n)]
    r1 = [lax.dot_general(kd[i], y[i], (((0,), (0,)), ((), ())), preferred_element_type=F32)
          for i in range(n)]
    stage_done()
    r2 = [jnp.dot(qk[i], y[i], preferred_element_type=F32) for i in range(n)]
    stage_done()
    assert done == [n_stages, len(hooks)]
    out = []
    for i in range(n):
        qp = qf[i] * eg[i] - r2[i][:, HEAD_DIM:]
        kq = jnp.concatenate([r1[i][:, HEAD_DIM:], qp], axis=0).astype(BF16)
        out.append((kq, r1[i][:, :HEAD_DIM], r2[i][:, :HEAD_DIM], jnp.exp(gl[i])))
    return out


GDN_HP = 2
GDN_CPI = 4
KQ_ROWS = HEAD_DIM + CHUNK


ATTN_KEY_BLOCK = 512


def _mixers_kernel(q_ref, k_ref, v_ref, z_ref, gt_ref, gtr_ref, nw_ref, qa_ref, ka_ref, va_ref,
                   o_ref, oa_ref, kq_ref, bm_ref, oc_ref, dc_ref):
    T = q_ref.shape[0]
    N = T // CHUNK
    n_it = N // GDN_CPI
    n_chains = 2 * GDN_HP
    hd0 = pl.program_id(1) * GDN_HP

    def chain_chunk(ch, m):
        return m if ch % 2 == 0 else N - 1 - m

    def local_part(it, hooks):
        probs, dest = [], []
        for ch in range(n_chains):
            hp, d = divmod(ch, 2)
            hd = hd0 + hp
            sl = slice(hp * HEAD_DIM, (hp + 1) * HEAD_DIM)
            for j in range(GDN_CPI):
                c = chain_chunk(ch, it * GDN_CPI + j)
                tok = pl.ds(pl.multiple_of(c * CHUNK, CHUNK), CHUNK)
                probs.append(dict(cid=(ch, j), q=q_ref[tok, sl], k=k_ref[tok, sl], v=v_ref[tok, sl],
                                  gtc=gt_ref[tok, :],
                                  gr=gtr_ref[c, pl.ds(d * LA_HEADS + hd, 1), :],
                                  lane_g=d * LA_HEADS + hd, lane_b=(2 + d) * LA_HEADS + hd,
                                  lower=(d == 0)))
                dest.append((ch * N + c, ch * T + c * CHUNK))
        for (kq, bm, o0, dec), (slot, orow) in zip(_chunk_local(probs, hooks), dest):
            kq_ref[pl.ds(pl.multiple_of(slot * KQ_ROWS, CHUNK), KQ_ROWS), :] = kq
            bm_ref[pl.ds(pl.multiple_of(slot * HEAD_DIM, HEAD_DIM), HEAD_DIM), :] = bm
            oc_ref[pl.ds(pl.multiple_of(orow, CHUNK), CHUNK), :] = o0
            dc_ref[pl.ds(pl.multiple_of(slot * 8, 8), 8), :] = jnp.broadcast_to(dec, (8, LANES))

    def state_step(m, states):
        slots = [ch * N + chain_chunk(ch, m) for ch in range(n_chains)]
        orows = [pl.ds(pl.multiple_of(ch * T + chain_chunk(ch, m) * CHUNK, CHUNK), CHUNK)
                 for ch in range(n_chains)]
        r = [jnp.dot(kq_ref[pl.ds(pl.multiple_of(slots[ch] * KQ_ROWS, CHUNK), KQ_ROWS), :],
                     states[ch].astype(BF16), preferred_element_type=F32)
             for ch in range(n_chains)]
        new = []
        for ch in range(n_chains):
            dec = dc_ref[pl.ds(pl.multiple_of(slots[ch] * 8, 8), 8), :][0:1, :]
            bm = bm_ref[pl.ds(pl.multiple_of(slots[ch] * HEAD_DIM, HEAD_DIM), HEAD_DIM), :]
            new.append(states[ch] * dec + bm - r[ch][:HEAD_DIM])
            oc_ref[orows[ch], :] = oc_ref[orows[ch], :] + r[ch][HEAD_DIM:]
        return new

    def attention_hooks(it):
        tq = qa_ref.shape[0] // n_it
        rows = pl.ds(pl.multiple_of(it * tq, tq), tq)
        n_kb = ka_ref.shape[0] // ATTN_KEY_BLOCK
        keys = lambda j: slice(j * ATTN_KEY_BLOCK, (j + 1) * ATTN_KEY_BLOCK)
        st = {}

        def scores(j):
            def run():
                if j == 0:
                    st["q"] = jnp.concatenate(
                        [qa_ref[rows, g * HEAD_DIM:(g + 1) * HEAD_DIM] for g in range(GQA_GROUP)],
                        axis=0)
                st[j] = _mm_nt(st["q"], ka_ref[keys(j), :])
            return run

        def exp_pv(j):
            def run():
                if j == 0:
                    m = jnp.max(st[0], axis=-1, keepdims=True)
                    for i in range(1, n_kb):
                        m = jnp.maximum(m, jnp.max(st[i], axis=-1, keepdims=True))
                    st["m"] = m
                p = jnp.exp2(st[j] - st["m"])
                lb = jnp.sum(p, axis=-1, keepdims=True)
                ob = _mm(p, va_ref[keys(j), :])
                st["l"], st["o"] = (lb, ob) if j == 0 else (st["l"] + lb, st["o"] + ob)
            return run

        def finish():
            o = (st["o"] / st["l"]).astype(BF16)
            for g in range(GQA_GROUP):
                oa_ref[rows, g * HEAD_DIM:(g + 1) * HEAD_DIM] = o[g * tq:(g + 1) * tq]

        return ([scores(j) for j in range(n_kb)], [exp_pv(j) for j in range(n_kb)] + [finish])

    def weave(first, second):
        out, j = [], 0
        for i, h in enumerate(first):
            out.append(h)
            while j < len(second) and (j + 1) * len(first) <= (i + 1) * len(second):
                out.append(second[j])
                j += 1
        return out + second[j:]

    sc, pv = attention_hooks(0)
    local_part(0, sc + pv)

    def body(it, states):
        st = [list(states)]

        def hook(j):
            def run():
                st[0] = state_step((it - 1) * GDN_CPI + j, st[0])
            return run

        sc, pv = attention_hooks(it)
        local_part(it, weave(sc + pv, [hook(j) for j in range(GDN_CPI)]))
        return tuple(st[0])

    zero = jnp.zeros((HEAD_DIM, HEAD_DIM), F32)
    states = list(lax.fori_loop(1, n_it, body, (zero,) * n_chains))
    for j in range(GDN_CPI):
        states = state_step((n_it - 1) * GDN_CPI + j, states)
    for hp in range(GDN_HP):
        sl = slice(hp * HEAD_DIM, (hp + 1) * HEAD_DIM)
        o = oc_ref[2 * hp * T:(2 * hp + 1) * T, :] + oc_ref[(2 * hp + 1) * T:(2 * hp + 2) * T, :]
        o_ref[:, sl] = (_rms(o, nw_ref[...]) * _silu(z_ref[:, sl])).astype(BF16)


def _mixers(qkv, z, gt, gtr, nw, qb, kb, vb):
    B, T, _ = qkv.shape
    N = T // CHUNK
    hw = GDN_HP * HEAD_DIM
    nblk = LA_HEADS // GDN_HP
    assert nblk % KV_HEADS == 0
    per_kv = nblk // KV_HEADS
    n_it = N // GDN_CPI
    ta = T // per_kv
    assert T % per_kv == 0 and ta % n_it == 0 and (ta // n_it) % 16 == 0
    assert T % ATTN_KEY_BLOCK == 0
    gw = GQA_GROUP * HEAD_DIM
    seq = lambda off: pl.BlockSpec((None, T, hw), lambda b, h: (b, 0, off + h))
    att_q = pl.BlockSpec((None, ta, gw), lambda b, h: (b, h % per_kv, h // per_kv))
    att_kv = pl.BlockSpec((None, T, HEAD_DIM), lambda b, h: (b, 0, h // per_kv))
    chains = 2 * GDN_HP
    return pl.pallas_call(
        _mixers_kernel,
        grid=(B, nblk),
        in_specs=[seq(0), seq(nblk), seq(2 * nblk),
                  seq(0),
                  pl.BlockSpec((None, T, LANES), lambda b, h: (b, 0, 0)),
                  pl.BlockSpec((None, N, 4 * LA_HEADS, LANES), lambda b, h: (b, 0, 0, 0)),
                  pl.BlockSpec((1, HEAD_DIM), lambda b, h: (0, 0)),
                  att_q, att_kv, att_kv],
        out_specs=[seq(0), att_q],
        out_shape=[jax.ShapeDtypeStruct((B, T, LA_DIM), BF16),
                   jax.ShapeDtypeStruct((B, T, ATT_Q_DIM), BF16)],
        scratch_shapes=[
            pltpu.VMEM((chains * N * KQ_ROWS, HEAD_DIM), BF16),
            pltpu.VMEM((chains * N * HEAD_DIM, HEAD_DIM), F32),
            pltpu.VMEM((chains * T, HEAD_DIM), F32),
            pltpu.VMEM((chains * N * 8, LANES), F32)],
        compiler_params=_params(2),
        name="mixers",
    )(qkv, qkv, qkv, z, gt, gtr, nw, qb, kb, vb)


def _merge_memattn_kernel(x_ref, oa_ref, ob_ref, gate_ref, wa_ref, wb_ref, wo_ref, nw_ref,
                          kv_ref, prew_ref, wq_ref, wmo_ref, postw_ref, o_ref, att_ref):
    ya = jnp.dot(oa_ref[...], wa_ref[...], preferred_element_type=F32)
    yb = jnp.dot(ob_ref[...], wb_ref[...], preferred_element_type=F32)
    mix = _sigmoid(gate_ref[:, :D_MODEL]) * ya + _sigmoid(gate_ref[:, D_MODEL:]) * yb
    mixed = jnp.dot(mix.astype(BF16), wo_ref[...], preferred_element_type=F32)
    x = x_ref[...] + _rms(mixed, nw_ref[...])

    h = _rms(x, prew_ref[...]).astype(BF16)
    q = (jnp.dot(h, wq_ref[...], preferred_element_type=F32)
         * (MEM_HEAD_DIM ** -0.5 * LOG2_E)).astype(BF16)
    heads = [slice(hd * MEM_HEAD_DIM, (hd + 1) * MEM_HEAD_DIM) for hd in range(MEM_HEADS)]
    scores = lambda sl: _mm_nt(q[:, sl], kv_ref[:, sl])
    s_next = scores(heads[0])
    for hd, sl in enumerate(heads):
        s = s_next
        if hd + 1 < MEM_HEADS:
            s_next = scores(heads[hd + 1])
        p = jnp.exp2(s - jnp.max(s, axis=-1, keepdims=True))
        l = jnp.sum(p, axis=-1, keepdims=True)
        v = kv_ref[:, D_MODEL + hd * MEM_HEAD_DIM:D_MODEL + (hd + 1) * MEM_HEAD_DIM]
        att_ref[:, sl] = (_mm(p, v) / l).astype(BF16)
    y = jnp.dot(att_ref[...], wmo_ref[...], preferred_element_type=F32)
    o_ref[...] = x + _rms(y, postw_ref[...])


def _merge_memattn(x2, oa, ob, gate, wa, wb, wo, nw, kv, prew, wq, wmo, postw, T, tm):
    M = x2.shape[0]
    Mt = kv.shape[1]
    tpb = T // tm
    row = lambda n: pl.BlockSpec((tm, n), lambda i: (i, 0))
    sq = _resident((D_MODEL, D_MODEL))
    vec = _resident((1, D_MODEL))
    return pl.pallas_call(
        _merge_memattn_kernel,
        grid=(M // tm,),
        in_specs=[row(D_MODEL), row(D_MODEL), row(D_MODEL), row(2 * D_MODEL), sq, sq, sq, vec,
                  pl.BlockSpec((None, Mt, 2 * D_MODEL), lambda i: (i // tpb, 0, 0)),
                  vec, sq, sq, vec],
        out_specs=row(D_MODEL),
        out_shape=jax.ShapeDtypeStruct((M, D_MODEL), F32),
        scratch_shapes=[pltpu.VMEM((tm, D_MODEL), BF16)],
        compiler_params=_params(1),
        name="merge_memattn",
    )(x2, oa, ob, gate, wa, wb, wo, nw, kv, prew, wq, wmo, postw)


def _memkv_kernel(mem_ref, nw_ref, w_ref, o_ref):
    o_ref[...] = jnp.dot(_rms(mem_ref[...], nw_ref[...]).astype(BF16), w_ref[...],
                         preferred_element_type=F32).astype(BF16)


def _memkv(mem, nw, w):
    B, Mt, _ = mem.shape
    return pl.pallas_call(
        _memkv_kernel,
        grid=(B,),
        in_specs=[pl.BlockSpec((None, Mt, D_MODEL), lambda b: (b, 0, 0)),
                  _resident((1, D_MODEL)), _resident((D_MODEL, 2 * D_MODEL))],
        out_specs=pl.BlockSpec((None, Mt, 2 * D_MODEL), lambda b: (b, 0, 0)),
        out_shape=jax.ShapeDtypeStruct((B, Mt, 2 * D_MODEL), BF16),
        compiler_params=_params(1),
        name="memkv",
    )(mem, nw, w)


def _ffn_kernel(x_ref, prew_ref, wg_ref, wu_ref, wd_ref, postw_ref, o_ref, *, n_split):
    x = x_ref[...]
    h = _rms(x, prew_ref[...]).astype(BF16)
    d_ff = wg_ref.shape[1]
    cs = d_ff // n_split
    assert cs % LANES == 0 and cs * n_split == d_ff
    y = None
    for c in range(0, d_ff, cs):
        gate = jnp.dot(h, wg_ref[:, c:c + cs], preferred_element_type=F32)
        up = jnp.dot(h, wu_ref[:, c:c + cs], preferred_element_type=F32)
        part = jnp.dot((_silu(gate) * up).astype(BF16), wd_ref[c:c + cs, :],
                       preferred_element_type=F32)
        y = part if y is None else y + part
    o_ref[...] = x + _rms(y, postw_ref[...])


def _ffn(x2, prew, wg, wu, wd, postw, tm):
    M = x2.shape[0]
    d_ff = wg.shape[1]
    row = pl.BlockSpec((tm, D_MODEL), lambda i: (i, 0))
    vec = _resident((1, D_MODEL))
    return pl.pallas_call(
        functools.partial(_ffn_kernel, n_split=11),
        grid=(M // tm,),
        in_specs=[row, vec, _resident((D_MODEL, d_ff)), _resident((D_MODEL, d_ff)),
                  _resident((d_ff, D_MODEL)), vec],
        out_specs=row,
        out_shape=jax.ShapeDtypeStruct((M, D_MODEL), F32),
        compiler_params=_params(1),
        name="ffn",
    )(x2, prew, wg, wu, wd, postw)


def _rope_tables(T):
    rows = T // GRID_W
    row = jnp.repeat(jnp.arange(rows, dtype=F32), GRID_W)
    col = jnp.tile(jnp.arange(GRID_W, dtype=F32), rows)
    half = HEAD_DIM // 2
    inv_freq = ROPE_THETA ** (-jnp.arange(0, half, 2, dtype=F32) / half)
    ang_r = row[:, None] * inv_freq
    ang_c = col[:, None] * inv_freq
    cos_t = jnp.concatenate([jnp.cos(ang_r), jnp.cos(ang_r), jnp.cos(ang_c), jnp.cos(ang_c)], axis=-1)
    sin_t = jnp.concatenate([-jnp.sin(ang_r), jnp.sin(ang_r), -jnp.sin(ang_c), jnp.sin(ang_c)], axis=-1)
    return cos_t, sin_t


def _layer(x, mem, mix_pre_norm, w_in, conv_w, la_a_log, la_dt_bias, la_norm_w, w_out_a,
           q_norm_w, k_norm_w, w_out_b, b_gate, w_out, mix_post_norm,
           mem_pre_norm, mem_kv_norm, w_mq, w_mkv, w_mo, mem_post_norm,
           ffn_pre_norm, w_ffn_in, w_ffn_out, ffn_post_norm):
    B, T, D = x.shape
    assert D == D_MODEL and T % 256 == 0 and T % GRID_W == 0
    M = B * T
    N = T // CHUNK
    tm_proj = 256
    tm_tok = 512 if T % 512 == 0 else 256
    vec = lambda a: a.reshape(1, -1).astype(F32)

    a0 = 4 * LA_DIM
    b0 = a0 + 4 * LA_HEADS
    w_r = jnp.concatenate([w_in[:, :a0], w_in[:, b0:], w_in[:, a0:b0],
                           jnp.zeros((D, LANES - 4 * LA_HEADS), w_in.dtype)], axis=1).astype(BF16)
    pad16 = lambda a: jnp.concatenate([a.reshape(-1).astype(F32), jnp.zeros((LANES - 2 * LA_HEADS,), F32)]).reshape(1, LANES)
    cos_t, sin_t = _rope_tables(T)

    x2 = x.reshape(M, D)
    qkv, z, qb, kb, vb, gate, gt = _proj(x2, vec(mix_pre_norm), w_r, conv_w.astype(F32), vec(b_gate),
                                         vec(q_norm_w), vec(k_norm_w), cos_t, sin_t,
                                         pad16(la_a_log), pad16(la_dt_bias), T, tm_proj)
    gt3 = gt.reshape(B, T, LANES)
    gtr = gt3[:, :, :4 * LA_HEADS].reshape(B, N, CHUNK, 4 * LA_HEADS).transpose(0, 1, 3, 2)
    gtr = jnp.concatenate([gtr, gtr], axis=-1)
    oa, ob = _mixers(qkv.reshape(B, T, 3 * LA_DIM), z.reshape(B, T, LA_DIM), gt3, gtr,
                     vec(la_norm_w), qb.reshape(B, T, ATT_Q_DIM), kb.reshape(B, T, ATT_KV_DIM),
                     vb.reshape(B, T, ATT_KV_DIM))
    kv = _memkv(mem, vec(mem_kv_norm), w_mkv.astype(BF16))
    x2 = _merge_memattn(x2, oa.reshape(M, LA_DIM), ob.reshape(M, ATT_Q_DIM), gate,
                        w_out_a.astype(BF16), w_out_b.astype(BF16), w_out.astype(BF16),
                        vec(mix_post_norm), kv, vec(mem_pre_norm), w_mq.astype(BF16),
                        w_mo.astype(BF16), vec(mem_post_norm), T, tm_tok)
    d_ff = w_ffn_out.shape[0]
    x2 = _ffn(x2, vec(ffn_pre_norm), w_ffn_in[:, :d_ff].astype(BF16), w_ffn_in[:, d_ff:].astype(BF16),
              w_ffn_out.astype(BF16), vec(ffn_post_norm), tm_tok)
    return x2.reshape(B, T, D)


def kernel(x, mem, mix_pre_norm, w_in, conv_w, la_a_log, la_dt_bias, la_norm_w, w_out_a, q_norm_w, k_norm_w, w_out_b, b_gate, w_out, mix_post_norm, mem_pre_norm, mem_kv_norm, w_mq, w_mkv, w_mo, mem_post_norm, ffn_pre_norm, w_ffn_in, w_ffn_out, ffn_post_norm):
    depth = w_in.shape[0]
    for l in range(depth):
        x = _layer(x, mem, mix_pre_norm[l], w_in[l], conv_w[l], la_a_log[l], la_dt_bias[l],
                   la_norm_w[l], w_out_a[l], q_norm_w[l], k_norm_w[l], w_out_b[l], b_gate[l],
                   w_out[l], mix_post_norm[l], mem_pre_norm[l], mem_kv_norm[l], w_mq[l], w_mkv[l],
                   w_mo[l], mem_post_norm[l], ffn_pre_norm[l], w_ffn_in[l], w_ffn_out[l],
                   ffn_post_norm[l])
    return x
```
